```python
import math
import jax, jax.numpy as jnp
from jax import lax
import numpy as np

D_MODEL = 1024
BATCH = 2
SEQ = 8192
DEPTH = 1

HG_WIDTH = 512
HG_HEAD_DIM = 128
HG_HEADS = HG_WIDTH // HG_HEAD_DIM
HG_CHUNK = 64
S5_WIDTH = 512
S5_GROUP = 16
S5_GROUPS = S5_WIDTH // S5_GROUP
S5_STATE = 64
DT_MIN = 1e-3
DT_MAX = 1e-1
D_FF = 2816
CONV_WIDTH = 3
PLE_DIM = 256
N_BRANCH = 2
N_IN = 4 * HG_WIDTH + S5_WIDTH + N_BRANCH * D_MODEL
SPLITS = (HG_WIDTH, 2 * HG_WIDTH, 3 * HG_WIDTH, 4 * HG_WIDTH,
          4 * HG_WIDTH + S5_WIDTH, 4 * HG_WIDTH + S5_WIDTH + D_MODEL)
EPS = 1e-6

kernel_name = "hybrid_hgrn2_s5_gated_block"


def _rmsnorm(x, g):
    xf = x.astype(jnp.float32)
    y = xf * lax.rsqrt(jnp.mean(xf * xf, axis=-1, keepdims=True) + EPS) * g.astype(jnp.float32)
    return y.astype(x.dtype)


def _hgrn2(q_raw, f_raw, i_raw, og_raw, lb, norm_g):
    bsz, seqlen, _ = q_raw.shape
    nc = seqlen // HG_CHUNK

    def heads(t):
        t = t.astype(jnp.float32).reshape(bsz, nc, HG_CHUNK, HG_HEADS, HG_HEAD_DIM)
        return t.transpose(0, 3, 1, 2, 4)

    lb = lb.astype(jnp.float32).reshape(HG_HEADS, 1, 1, HG_HEAD_DIM)
    f = lb + (1.0 - lb) * jax.nn.sigmoid(heads(f_raw))
    k = 1.0 - f
    q = jax.nn.silu(heads(q_raw))
    v = heads(i_raw)
    b = jnp.cumsum(jnp.log(f), axis=-2)
    b_ref = b[..., HG_CHUNK // 2:HG_CHUNK // 2 + 1, :]
    b_last = b[..., -1:, :]
    scores = jnp.einsum('bhncd,bhnsd->bhncs', q * jnp.exp(b - b_ref), k * jnp.exp(b_ref - b))
    causal = jnp.tril(jnp.ones((HG_CHUNK, HG_CHUNK), dtype=bool))
    scores = jnp.where(causal, scores, 0.0)
    o_intra = jnp.einsum('bhncs,bhnse->bhnce', scores, v)
    u_chunk = jnp.einsum('bhncd,bhnce->bhnde', k * jnp.exp(b_last - b), v)
    decay = jnp.exp(b_last[..., 0, :])

    def step(S, inp):
        dec, u = inp
        return dec[..., None] * S + u, S

    S0 = jnp.zeros((bsz, HG_HEADS, HG_HEAD_DIM, HG_HEAD_DIM), jnp.float32)
    _, S_prev = lax.scan(step, S0, (jnp.moveaxis(decay, 2, 0), jnp.moveaxis(u_chunk, 2, 0)))
    S_prev = jnp.moveaxis(S_prev, 0, 2)
    o_inter = jnp.einsum('bhncd,bhnde->bhnce', q * jnp.exp(b), S_prev)
    o = o_intra + o_inter
    o = o * lax.rsqrt(jnp.mean(o * o, axis=-1, keepdims=True) + EPS) * norm_g.astype(jnp.float32)
    o = o.transpose(0, 2, 3, 1, 4).reshape(bsz, seqlen, HG_WIDTH)
    return (o * jax.nn.silu(og_raw.astype(jnp.float32))).astype(q_raw.dtype)


def _complex_affine_combine(e1, e2):
    a1r, a1i, b1r, b1i = e1
    a2r, a2i, b2r, b2i = e2
    return (a2r * a1r - a2i * a1i,
            a2r * a1i + a2i * a1r,
            a2r * b1r - a2i * b1i + b2r,
            a2r * b1i + a2i * b1r + b2i)


def _s5(u_raw, lam_re, lam_im, log_dt, b_re, b_im, c_re, c_im, d_skip, glu_w, glu_b):
    bsz, seqlen, _ = u_raw.shape
    u = u_raw.astype(jnp.float32).reshape(bsz, seqlen, S5_GROUPS, S5_GROUP)
    lr = lam_re.astype(jnp.float32)
    li = lam_im.astype(jnp.float32)
    dt = jnp.exp(log_dt.astype(jnp.float32))[:, None]
    mag = jnp.exp(lr * dt)
    a_re = mag * jnp.cos(li * dt)
    a_im = mag * jnp.sin(li * dt)
    den = lr * lr + li * li
    coef_re = ((a_re - 1.0) * lr + a_im * li) / den
    coef_im = (a_im * lr - (a_re - 1.0) * li) / den
    br = b_re.astype(jnp.float32)
    bi = b_im.astype(jnp.float32)
    bbar_re = coef_re[..., None] * br - coef_im[..., None] * bi
    bbar_im = coef_re[..., None] * bi + coef_im[..., None] * br
    bu_re = jnp.einsum('gnp,blgp->blgn', bbar_re, u)
    bu_im = jnp.einsum('gnp,blgp->blgn', bbar_im, u)
    shape = bu_re.shape
    elems = (jnp.broadcast_to(a_re, shape), jnp.broadcast_to(a_im, shape), bu_re, bu_im)
    _, _, x_re, x_im = lax.associative_scan(_complex_affine_combine, elems, axis=1)
    y = (jnp.einsum('gpn,blgn->blgp', c_re.astype(jnp.float32), x_re)
         - jnp.einsum('gpn,blgn->blgp', c_im.astype(jnp.float32), x_im))
    y = y + d_skip.astype(jnp.float32).reshape(S5_GROUPS, S5_GROUP) * u
    y = y.reshape(bsz, seqlen, S5_WIDTH).astype(u_raw.dtype)
    g = jax.nn.gelu(y)
    return g * jax.nn.sigmoid(g @ glu_w + glu_b)


def _conv_gated_ffn(h, w_up, conv_w, conv_b, w_down):
    a = h @ w_up
    seqlen = a.shape[1]
    ap = jnp.pad(a, ((0, 0), (CONV_WIDTH - 1, 0), (0, 0)))
    c = conv_b + conv_w[0] * ap[:, 0:seqlen]
    for k in range(1, CONV_WIDTH):
        c = c + conv_w[k] * ap[:, k:k + seqlen]
    gate, val = jnp.split(c, 2, axis=-1)
    return (jax.nn.gelu(gate) * val) @ w_down


def setup_inputs(seed: int = 0) -> dict:
    key = jax.random.key(seed)
    ks = jax.random.split(key, 32)
    f32 = jnp.float32

    def nrm(k, shape, scale):
        return jax.random.normal(k, shape, f32) * scale

    def gain(k, shape):
        return 1.0 + 0.01 * jax.random.normal(k, shape, f32)

    n_idx = jnp.arange(S5_STATE, dtype=f32)
    lam_re = -0.5 + 0.01 * jax.random.normal(ks[4], (DEPTH, S5_GROUPS, S5_STATE), f32)
    lam_im = math.pi * n_idx + 0.01 * jax.random.normal(ks[5], (DEPTH, S5_GROUPS, S5_STATE), f32)
    log_dt = jax.random.uniform(ks[6], (DEPTH, S5_GROUPS), f32,
                                minval=math.log(DT_MIN), maxval=math.log(DT_MAX))
    return {
        "x": nrm(ks[0], (BATCH, SEQ, D_MODEL), 1.0),
        "p": nrm(ks[1], (DEPTH, BATCH, SEQ, PLE_DIM), 1.0),
        "norm_mix_g": gain(ks[2], (DEPTH, D_MODEL)),
        "w_in": nrm(ks[3], (DEPTH, D_MODEL, N_IN), D_MODEL ** -0.5),
        "hg_lb_logits": nrm(ks[7], (DEPTH + 1, HG_WIDTH), 0.1),
        "hg_norm_g": gain(ks[8], (DEPTH, HG_HEAD_DIM)),
        "s5_lambda_re": lam_re,
        "s5_lambda_im": lam_im,
        "s5_log_dt": log_dt,
        "s5_b_re": nrm(ks[9], (DEPTH, S5_GROUPS, S5_STATE, S5_GROUP), (2 * S5_GROUP) ** -0.5),
        "s5_b_im": nrm(ks[10], (DEPTH, S5_GROUPS, S5_STATE, S5_GROUP), (2 * S5_GROUP) ** -0.5),
        "s5_c_re": nrm(ks[11], (DEPTH, S5_GROUPS, S5_GROUP, S5_STATE), S5_STATE ** -0.5),
        "s5_c_im": nrm(ks[12], (DEPTH, S5_GROUPS, S5_GROUP, S5_STATE), S5_STATE ** -0.5),
        "s5_d": nrm(ks[13], (DEPTH, S5_WIDTH), 1.0),
        "s5_glu_w": nrm(ks[14], (DEPTH, S5_WIDTH, S5_WIDTH), S5_WIDTH ** -0.5),
        "s5_glu_b": nrm(ks[15], (DEPTH, S5_WIDTH), 0.01),
        "w_branch_hg": nrm(ks[16], (DEPTH, HG_WIDTH, D_MODEL), HG_WIDTH ** -0.5),
        "w_branch_s5": nrm(ks[17], (DEPTH, S5_WIDTH, D_MODEL), S5_WIDTH ** -0.5),
        "w_out": nrm(ks[18], (DEPTH, D_MODEL, D_MODEL), D_MODEL ** -0.5),
        "norm_ffn_g": gain(ks[19], (DEPTH, D_MODEL)),
        "w_up": nrm(ks[20], (DEPTH, D_MODEL, 2 * D_FF), D_MODEL ** -0.5),
        "conv_w": nrm(ks[21], (DEPTH, CONV_WIDTH, 2 * D_FF), CONV_WIDTH ** -0.5),
        "conv_b": nrm(ks[22], (DEPTH, 2 * D_FF), 0.01),
        "w_down": nrm(ks[23], (DEPTH, D_FF, D_MODEL), D_FF ** -0.5),
        "norm_ple_g": gain(ks[24], (DEPTH, D_MODEL)),
        "w_ple_gate": nrm(ks[25], (DEPTH, D_MODEL, D_MODEL), D_MODEL ** -0.5),
        "w_ple_proj": nrm(ks[26], (DEPTH, PLE_DIM, D_MODEL), PLE_DIM ** -0.5),
        "norm_final_g": gain(ks[27], (D_MODEL,)),
    }


def reference(x, p, norm_mix_g, w_in, hg_lb_logits, hg_norm_g, s5_lambda_re, s5_lambda_im,
              s5_log_dt, s5_b_re, s5_b_im, s5_c_re, s5_c_im, s5_d, s5_glu_w, s5_glu_b,
              w_branch_hg, w_branch_s5, w_out, norm_ffn_g, w_up, conv_w, conv_b, w_down,
              norm_ple_g, w_ple_gate, w_ple_proj, norm_final_g):
    lbs = jnp.cumsum(jax.nn.softmax(hg_lb_logits.astype(jnp.float32), axis=0), axis=0)
    for i in range(DEPTH):
        h = _rmsnorm(x, norm_mix_g[i])
        proj = h @ w_in[i]
        q_raw, f_raw, i_raw, og_raw, u_raw, gate_hg, gate_s5 = jnp.split(proj, SPLITS, axis=-1)
        y_hg = _hgrn2(q_raw, f_raw, i_raw, og_raw, lbs[i], hg_norm_g[i]) @ w_branch_hg[i]
        y_s5 = _s5(u_raw, s5_lambda_re[i], s5_lambda_im[i], s5_log_dt[i], s5_b_re[i], s5_b_im[i],
                   s5_c_re[i], s5_c_im[i], s5_d[i], s5_glu_w[i], s5_glu_b[i]) @ w_branch_s5[i]
        merged = jax.nn.sigmoid(gate_hg) * y_hg + jax.nn.sigmoid(gate_s5) * y_s5
        x = x + merged @ w_out[i]
        x = x + _conv_gated_ffn(_rmsnorm(x, norm_ffn_g[i]), w_up[i], conv_w[i], conv_b[i], w_down[i])
        ple_gate = jax.nn.sigmoid(_rmsnorm(x, norm_ple_g[i]) @ w_ple_gate[i])
        x = x + ple_gate * (p[i] @ w_ple_proj[i])
    return _rmsnorm(x, norm_final_g)
```

```python
import functools

import jax
import jax.numpy as jnp
from jax import lax
from jax.experimental import pallas as pl
from jax.experimental.pallas import tpu as pltpu

F32 = jnp.float32
BF16 = jnp.bfloat16

EPS = 1e-6
HG_CHUNK = 64
HG_HEAD_DIM = 128
S5_GROUP = 16
S5_STATE = 64
CONV_WIDTH = 3

SUBLANES = 8
LANES = 128
VMEM_LIMIT_BYTES = 56 * 1024 * 1024

HG_TILE = 512
S5_TILE = 512
MIX_TILE = 512
FFN_TILE = 256
S5_KBLK = 128
S5_NBLK = S5_KBLK // S5_GROUP * S5_STATE
CONV_HALO = SUBLANES


def _rmsnorm(x, g):
    return x * lax.rsqrt(jnp.mean(x * x, axis=-1, keepdims=True) + EPS) * g


def _dot(a, b):
    return jnp.dot(a, b, preferred_element_type=F32)


def _const_spec(shape):
    nd = len(shape)
    return pl.BlockSpec(shape, lambda *_: (0,) * nd)


def _s5_prep_kernel(lr_ref, li_ref, ldt_ref, br_ref, bi_ref,
                    bbr_ref, bbi_ref, seed_ref, dbl_ref):
    lr = lr_ref[...]
    li = li_ref[...]
    dt = jnp.exp(ldt_ref[...])
    mag = jnp.exp(lr * dt)
    a_re = mag * jnp.cos(li * dt)
    a_im = mag * jnp.sin(li * dt)
    den = lr * lr + li * li
    coef_re = ((a_re - 1.0) * lr + a_im * li) / den
    coef_im = (a_im * lr - (a_re - 1.0) * li) / den
    nblk = br_ref.shape[0]
    for j in range(nblk):
        cr = coef_re[:, j * S5_NBLK:(j + 1) * S5_NBLK]
        ci = coef_im[:, j * S5_NBLK:(j + 1) * S5_NBLK]
        br = br_ref[j]
        bi = bi_ref[j]
        bbr_ref[j] = (cr * br - ci * bi).astype(BF16)
        bbi_ref[j] = (cr * bi + ci * br).astype(BF16)
    gn = lr.shape[-1]
    row = lax.broadcasted_iota(jnp.int32, (SUBLANES, gn), 0)
    seed_ref[0] = jnp.where(row == 0, a_re, 0.0)
    seed_ref[1] = jnp.where(row == 0, a_im, 0.0)
    pr, pi = a_re, a_im
    for s in range(3):
        k = 1 << s
        dbl_ref[s, 0] = jnp.where(row >= k, pr, 0.0)
        dbl_ref[s, 1] = jnp.where(row >= k, pi, 0.0)
        pr, pi = pr * pr - pi * pi, 2.0 * pr * pi


def _s5_prep(lam_re, lam_im, log_dt, b_re, b_im):
    g, n = lam_re.shape
    gn = g * n
    p = b_re.shape[-1]
    gpb = S5_KBLK // p
    nblk = g // gpb
    lr = lam_re.reshape(1, gn)
    li = lam_im.reshape(1, gn)
    ldt = jnp.repeat(log_dt, n).reshape(1, gn)
    eye = jnp.eye(gpb, dtype=F32)

    def blockdiag(b):
        bt = jnp.transpose(b, (0, 2, 1)).reshape(nblk, gpb, p, n)
        return jnp.einsum('jgqn,gh->jgqhn', bt, eye).reshape(nblk, gpb * p, gpb * n)

    out_shape = (
        jax.ShapeDtypeStruct((nblk, S5_KBLK, S5_NBLK), BF16),
        jax.ShapeDtypeStruct((nblk, S5_KBLK, S5_NBLK), BF16),
        jax.ShapeDtypeStruct((2, SUBLANES, gn), F32),
        jax.ShapeDtypeStruct((3, 2, SUBLANES, gn), F32),
    )
    return pl.pallas_call(_s5_prep_kernel, out_shape=out_shape, name="s5_prep")(
        lr, li, ldt, blockdiag(b_re), blockdiag(b_im))


def _hg_kernel(layer, x_ref, g_ref, w_ref, lbl_ref, ng_ref, o_ref, st_ref):
    @pl.when(pl.program_id(1) == 0)
    def _():
        st_ref[...] = jnp.zeros_like(st_ref)

    rows = x_ref.shape[0]
    nch = rows // HG_CHUNK
    width = o_ref.shape[-1]
    heads = width // HG_HEAD_DIM

    h = _rmsnorm(x_ref[...], g_ref[...]).astype(BF16)
    proj = _dot(h, w_ref[...])
    q_raw = proj[:, 0:width]
    f_raw = proj[:, width:2 * width]
    v = proj[:, 2 * width:3 * width]
    og = proj[:, 3 * width:4 * width]

    lb = jnp.sum(jax.nn.softmax(lbl_ref[...], axis=0)[0:layer + 1], axis=0, keepdims=True)
    f = lb + (1.0 - lb) * jax.nn.sigmoid(f_raw)
    k = 1.0 - f
    q = jax.nn.silu(q_raw)
    logf = jnp.log(f)

    r_i = lax.broadcasted_iota(jnp.int32, (HG_CHUNK, HG_CHUNK), 0)
    c_i = lax.broadcasted_iota(jnp.int32, (HG_CHUNK, HG_CHUNK), 1)
    causal = r_i >= c_i
    tri = jnp.broadcast_to(jnp.where(causal, 1.0, 0.0).astype(BF16), (nch, HG_CHUNK, HG_CHUNK))
    hi = logf.astype(BF16)
    r1 = logf - hi.astype(F32)
    mid = r1.astype(BF16)
    lo = (r1 - mid.astype(F32)).astype(BF16)
    b = None
    for part in (hi, mid, lo):
        t = jnp.einsum('cts,csn->ctn', tri, part.reshape(nch, HG_CHUNK, width),
                       preferred_element_type=F32)
        b = t if b is None else b + t

    b_mid = b[:, HG_CHUNK // 2:HG_CHUNK // 2 + 1, :]
    b_last = b[:, HG_CHUNK - 1:HG_CHUNK, :]
    q3 = q.reshape(nch, HG_CHUNK, width)
    k3 = k.reshape(nch, HG_CHUNK, width)
    v3 = v.reshape(nch, HG_CHUNK, width).astype(BF16)
    qe = (q3 * jnp.exp(b - b_mid)).astype(BF16)
    ke = (k3 * jnp.exp(b_mid - b)).astype(BF16)
    kl = (k3 * jnp.exp(b_last - b)).astype(BF16)
    qb = (q3 * jnp.exp(b)).astype(BF16)
    dec = jnp.exp(b_last)

    ng = ng_ref[...]
    for hd in range(heads):
        sl = slice(hd * HG_HEAD_DIM, (hd + 1) * HG_HEAD_DIM)
        s = jnp.einsum('cqd,ckd->cqk', qe[:, :, sl], ke[:, :, sl], preferred_element_type=F32)
        s = jnp.where(causal[None], s, 0.0)
        o_intra = jnp.einsum('cqk,cke->cqe', s.astype(BF16), v3[:, :, sl], preferred_element_type=F32)
        ut = jnp.einsum('cse,csd->ced', v3[:, :, sl], kl[:, :, sl], preferred_element_type=F32)
        st = st_ref[hd]
        outs = []
        for c in range(nch):
            o_inter = lax.dot_general(qb[c, :, sl], st.astype(BF16), (((1,), (1,)), ((), ())),
                                      preferred_element_type=F32)
            outs.append(o_intra[c] + o_inter)
            st = dec[c, :, sl] * st + ut[c]
        st_ref[hd] = st
        o = jnp.concatenate(outs, axis=0)
        o = o * lax.rsqrt(jnp.mean(o * o, axis=-1, keepdims=True) + EPS) * ng
        o_ref[:, sl] = (o * jax.nn.silu(og[:, sl])).astype(o_ref.dtype)


def _hg_mixer(layer, x, g_mix, w_hg, lb_logits, norm_g):
    bsz, seqlen, d = x.shape
    width = w_hg.shape[1] // 4
    heads = width // HG_HEAD_DIM
    ng = norm_g.reshape(1, HG_HEAD_DIM)
    grid = (bsz, seqlen // HG_TILE)
    return pl.pallas_call(
        functools.partial(_hg_kernel, layer),
        out_shape=jax.ShapeDtypeStruct((bsz, seqlen, width), BF16),
        grid=grid,
        in_specs=[
            pl.BlockSpec((None, HG_TILE, d), lambda b, t: (b, t, 0)),
            _const_spec((1, d)),
            _const_spec(w_hg.shape),
            _const_spec(lb_logits.shape),
            _const_spec((1, HG_HEAD_DIM)),
        ],
        out_specs=pl.BlockSpec((None, HG_TILE, width), lambda b, t: (b, t, 0)),
        scratch_shapes=[pltpu.VMEM((heads, HG_HEAD_DIM, HG_HEAD_DIM), F32)],
        compiler_params=pltpu.CompilerParams(
            dimension_semantics=("arbitrary", "arbitrary"), vmem_limit_bytes=VMEM_LIMIT_BYTES),
        name="hg_mixer",
    )(x, g_mix.reshape(1, d), w_hg, lb_logits, ng)


def _s5_kernel(x_ref, g_ref, wu_ref, bbr_ref, bbi_ref, seed_ref, dbl_ref, cre_ref, cim_ref,
               dskip_ref, gw_ref, gb_ref, o_ref, xr_ref, xi_ref, car_ref):
    @pl.when(pl.program_id(1) == 0)
    def _():
        car_ref[...] = jnp.zeros_like(car_ref)

    rows = x_ref.shape[0]
    gn = xr_ref.shape[-1]
    nblk = bbr_ref.shape[0]
    ncol = gn // LANES

    h = _rmsnorm(x_ref[...], g_ref[...]).astype(BF16)
    u = _dot(h, wu_ref[...])
    ub = u.astype(BF16)
    for j in range(nblk):
        uj = ub[:, j * S5_KBLK:(j + 1) * S5_KBLK]
        xr_ref[:, j * S5_NBLK:(j + 1) * S5_NBLK] = _dot(uj, bbr_ref[j])
        xi_ref[:, j * S5_NBLK:(j + 1) * S5_NBLK] = _dot(uj, bbi_ref[j])

    def row_block(rb, carry):
        r0 = pl.multiple_of(rb * SUBLANES, SUBLANES)
        new = []
        for c in range(ncol):
            cs = slice(c * LANES, (c + 1) * LANES)
            pr, pi = carry[2 * c], carry[2 * c + 1]
            re = xr_ref[pl.ds(r0, SUBLANES), cs]
            im = xi_ref[pl.ds(r0, SUBLANES), cs]
            sr, si = seed_ref[0, :, cs], seed_ref[1, :, cs]
            qr = pltpu.roll(pr, 1, 0)
            qi = pltpu.roll(pi, 1, 0)
            re, im = re + sr * qr - si * qi, im + sr * qi + si * qr
            for s in range(3):
                ar, ai = dbl_ref[s, 0, :, cs], dbl_ref[s, 1, :, cs]
                qr = pltpu.roll(re, 1 << s, 0)
                qi = pltpu.roll(im, 1 << s, 0)
                re, im = re + ar * qr - ai * qi, im + ar * qi + ai * qr
            xr_ref[pl.ds(r0, SUBLANES), cs] = re
            xi_ref[pl.ds(r0, SUBLANES), cs] = im
            new += [re, im]
        return tuple(new)

    init = tuple(car_ref[i] for i in range(2 * ncol))
    fin = lax.fori_loop(0, rows // SUBLANES, row_block, init)
    for i in range(2 * ncol):
        car_ref[i] = fin[i]

    nout = cre_ref.shape[0]
    kout = cre_ref.shape[1]
    ys = []
    for m in range(nout):
        ks = slice(m * kout, (m + 1) * kout)
        ys.append(_dot(xr_ref[:, ks].astype(BF16), cre_ref[m])
                  - _dot(xi_ref[:, ks].astype(BF16), cim_ref[m]))
    y = jnp.concatenate(ys, axis=-1) + dskip_ref[...] * u
    gl = jax.nn.gelu(y)
    o_ref[...] = (gl * jax.nn.sigmoid(_dot(gl.astype(BF16), gw_ref[...]) + gb_ref[...])).astype(o_ref.dtype)


def _s5_mixer(x, g_mix, w_u, bbr, bbi, seed, dbl, c_re, c_im, d_skip, glu_w, glu_b):
    bsz, seqlen, d = x.shape
    width = w_u.shape[1]
    gn = seed.shape[-1]
    g, p, n = c_re.shape
    ncol = gn // LANES
    gpo = 2 * LANES // p
    nout = g // gpo
    eye = jnp.eye(gpo, dtype=F32)

    def blockdiag(c):
        ct = jnp.transpose(c, (0, 2, 1)).reshape(nout, gpo, n, p)
        return jnp.einsum('mgnp,gh->mgnhp', ct, eye).reshape(nout, gpo * n, gpo * p).astype(BF16)

    grid = (bsz, seqlen // S5_TILE)
    return pl.pallas_call(
        _s5_kernel,
        out_shape=jax.ShapeDtypeStruct((bsz, seqlen, width), BF16),
        grid=grid,
        in_specs=[
            pl.BlockSpec((None, S5_TILE, d), lambda b, t: (b, t, 0)),
            _const_spec((1, d)),
            _const_spec(w_u.shape),
            _const_spec(bbr.shape),
            _const_spec(bbi.shape),
            _const_spec(seed.shape),
            _const_spec(dbl.shape),
            _const_spec((nout, gpo * n, gpo * p)),
            _const_spec((nout, gpo * n, gpo * p)),
            _const_spec((1, width)),
            _const_spec(glu_w.shape),
            _const_spec((1, width)),
        ],
        out_specs=pl.BlockSpec((None, S5_TILE, width), lambda b, t: (b, t, 0)),
        scratch_shapes=[
            pltpu.VMEM((S5_TILE, gn), F32),
            pltpu.VMEM((S5_TILE, gn), F32),
            pltpu.VMEM((2 * ncol, SUBLANES, LANES), F32),
        ],
        compiler_params=pltpu.CompilerParams(
            dimension_semantics=("arbitrary", "arbitrary"), vmem_limit_bytes=VMEM_LIMIT_BYTES),
        name="s5_mixer",
    )(x, g_mix.reshape(1, d), w_u, bbr, bbi, seed, dbl, blockdiag(c_re), blockdiag(c_im),
      d_skip.reshape(1, width), glu_w, glu_b.reshape(1, width))


def _merge_kernel(x_ref, g_ref, wg_ref, ohg_ref, os5_ref, whg_ref, ws5_ref, wout_ref, o_ref):
    d = x_ref.shape[-1]
    x = x_ref[...]
    h = _rmsnorm(x, g_ref[...]).astype(BF16)
    gates = _dot(h, wg_ref[...])
    y_hg = _dot(ohg_ref[...], whg_ref[...])
    y_s5 = _dot(os5_ref[...], ws5_ref[...])
    merged = jax.nn.sigmoid(gates[:, :d]) * y_hg + jax.nn.sigmoid(gates[:, d:]) * y_s5
    o_ref[...] = x + _dot(merged.astype(BF16), wout_ref[...])


def _merge(x2, g_mix, w_gates, o_hg, o_s5, w_bhg, w_bs5, w_out):
    t, d = x2.shape
    width = o_hg.shape[-1]
    return pl.pallas_call(
        _merge_kernel,
        out_shape=jax.ShapeDtypeStruct((t, d), F32),
        grid=(t // MIX_TILE,),
        in_specs=[
            pl.BlockSpec((MIX_TILE, d), lambda i: (i, 0)),
            _const_spec((1, d)),
            _const_spec(w_gates.shape),
            pl.BlockSpec((MIX_TILE, width), lambda i: (i, 0)),
            pl.BlockSpec((MIX_TILE, width), lambda i: (i, 0)),
            _const_spec(w_bhg.shape),
            _const_spec(w_bs5.shape),
            _const_spec(w_out.shape),
        ],
        out_specs=pl.BlockSpec((MIX_TILE, d), lambda i: (i, 0)),
        compiler_params=pltpu.CompilerParams(
            dimension_semantics=("arbitrary",), vmem_limit_bytes=VMEM_LIMIT_BYTES),
        name="merge",
    )(x2, g_mix.reshape(1, d), w_gates, o_hg, o_s5, w_bhg, w_bs5, w_out)


def _ffn_kernel(x_ref, p_ref, gf_ref, wup_ref, cw_ref, cb_ref, wdn_ref, gp_ref, wpg_ref, wpp_ref,
                gl_ref, o_ref, a_ref):
    rows = x_ref.shape[0]
    dff = wdn_ref.shape[0]

    @pl.when(pl.program_id(1) == 0)
    def _():
        a_ref[0:CONV_HALO, :] = jnp.zeros((CONV_HALO, a_ref.shape[-1]), F32)

    x = x_ref[...]
    h = _rmsnorm(x, gf_ref[...]).astype(BF16)
    a_ref[CONV_HALO:CONV_HALO + rows, :] = _dot(h, wup_ref[...])
    c = cb_ref[...]
    for k in range(CONV_WIDTH):
        off = CONV_HALO - (CONV_WIDTH - 1) + k
        c = c + cw_ref[k:k + 1, :] * a_ref[off:off + rows, :]
    a_ref[0:CONV_HALO, :] = a_ref[rows:rows + CONV_HALO, :]
    act = (jax.nn.gelu(c[:, :dff]) * c[:, dff:]).astype(BF16)
    x = x + _dot(act, wdn_ref[...])
    hp = _rmsnorm(x, gp_ref[...]).astype(BF16)
    ple_gate = jax.nn.sigmoid(_dot(hp, wpg_ref[...]))
    x = x + ple_gate * _dot(p_ref[...].astype(BF16), wpp_ref[...])
    o_ref[...] = _rmsnorm(x, gl_ref[...])


def _ffn(x1, p, g_ffn, w_up, conv_w, conv_b, w_down, g_ple, w_pg, w_pp, g_final):
    bsz, seqlen, d = x1.shape
    pdim = p.shape[-1]
    dff2 = w_up.shape[1]
    row_spec = pl.BlockSpec((None, FFN_TILE, d), lambda b, t: (b, t, 0))
    return pl.pallas_call(
        _ffn_kernel,
        out_shape=jax.ShapeDtypeStruct((bsz, seqlen, d), F32),
        grid=(bsz, seqlen // FFN_TILE),
        in_specs=[
            row_spec,
            pl.BlockSpec((None, FFN_TILE, pdim), lambda b, t: (b, t, 0)),
            _const_spec((1, d)),
            _const_spec(w_up.shape),
            _const_spec(conv_w.shape),
            _const_spec((1, dff2)),
            _const_spec(w_down.shape),
            _const_spec((1, d)),
            _const_spec(w_pg.shape),
            _const_spec(w_pp.shape),
            _const_spec((1, d)),
        ],
        out_specs=row_spec,
        scratch_shapes=[pltpu.VMEM((CONV_HALO + FFN_TILE, dff2), F32)],
        compiler_params=pltpu.CompilerParams(
            dimension_semantics=("arbitrary", "arbitrary"), vmem_limit_bytes=VMEM_LIMIT_BYTES),
        name="ffn",
    )(x1, p, g_ffn.reshape(1, d), w_up, conv_w, conv_b.reshape(1, dff2), w_down,
      g_ple.reshape(1, d), w_pg, w_pp, g_final.reshape(1, d))


def kernel(x, p, norm_mix_g, w_in, hg_lb_logits, hg_norm_g, s5_lambda_re, s5_lambda_im, s5_log_dt, s5_b_re, s5_b_im, s5_c_re, s5_c_im, s5_d, s5_glu_w, s5_glu_b, w_branch_hg, w_branch_s5, w_out, norm_ffn_g, w_up, conv_w, conv_b, w_down, norm_ple_g, w_ple_gate, w_ple_proj, norm_final_g):
    depth = w_in.shape[0]
    bsz, seqlen, d = x.shape
    hgw = w_branch_hg.shape[1]
    s5w = w_branch_s5.shape[1]
    for i in range(depth):
        w_in_b = w_in[i].astype(BF16)
        w_hg = w_in_b[:, :4 * hgw]
        w_u = w_in_b[:, 4 * hgw:4 * hgw + s5w]
        w_gates = w_in_b[:, 4 * hgw + s5w:]
        bbr, bbi, seed, dbl = _s5_prep(s5_lambda_re[i], s5_lambda_im[i], s5_log_dt[i],
                                       s5_b_re[i], s5_b_im[i])
        o_hg = _hg_mixer(i, x, norm_mix_g[i], w_hg, hg_lb_logits, hg_norm_g[i])
        o_s5 = _s5_mixer(x, norm_mix_g[i], w_u, bbr, bbi, seed, dbl, s5_c_re[i], s5_c_im[i],
                         s5_d[i], s5_glu_w[i].astype(BF16), s5_glu_b[i])
        x1 = _merge(x.reshape(bsz * seqlen, d), norm_mix_g[i], w_gates,
                    o_hg.reshape(bsz * seqlen, hgw), o_s5.reshape(bsz * seqlen, s5w),
                    w_branch_hg[i].astype(BF16), w_branch_s5[i].astype(BF16), w_out[i].astype(BF16))
        assert i == depth - 1, "the final norm is fused into the FFN call, so only one layer is supported"
        x = _ffn(x1.reshape(bsz, seqlen, d), p[i], norm_ffn_g[i], w_up[i].astype(BF16), conv_w[i],
                 conv_b[i], w_down[i].astype(BF16), norm_ple_g[i], w_ple_gate[i].astype(BF16),
                 w_ple_proj[i].astype(BF16), norm_final_g)
    return x
```

```python
import functools

import jax
import jax.numpy as jnp
from jax import lax
from jax.experimental import pallas as pl
from jax.experimental.pallas import tpu as pltpu

F32 = jnp.float32
BF16 = jnp.bfloat16

EPS = 1e-6
HG_CHUNK = 64
HG_HEAD_DIM = 128
S5_GROUP = 16
S5_STATE = 64
CONV_WIDTH = 3

SUBLANES = 8
LANES = 128
VMEM_LIMIT_BYTES = 56 * 1024 * 1024

HG_TILE = 512
S5_TILE = 512
S5_SEG = S5_TILE // SUBLANES
assert S5_SEG & (S5_SEG - 1) == 0, "Abar^S5_SEG is built by repeated squaring"
MIX_TILE = 512
FFN_TILE = 256
S5_KBLK = 128
S5_NBLK = S5_KBLK // S5_GROUP * S5_STATE
CONV_HALO = SUBLANES


def _rmsnorm(x, g):
    return x * lax.rsqrt(jnp.mean(x * x, axis=-1, keepdims=True) + EPS) * g


def _dot(a, b):
    return jnp.dot(a, b, preferred_element_type=F32)


def _const_spec(shape):
    nd = len(shape)
    return pl.BlockSpec(shape, lambda *_: (0,) * nd)


def _s5_prep_kernel(lr_ref, li_ref, ldt_ref, br_ref, bi_ref,
                    bbr_ref, bbi_ref, ab_ref, aseg_ref, dbl_ref):
    lr = lr_ref[...]
    li = li_ref[...]
    dt = jnp.exp(ldt_ref[...])
    mag = jnp.exp(lr * dt)
    a_re = mag * jnp.cos(li * dt)
    a_im = mag * jnp.sin(li * dt)
    den = lr * lr + li * li
    coef_re = ((a_re - 1.0) * lr + a_im * li) / den
    coef_im = (a_im * lr - (a_re - 1.0) * li) / den
    nblk = br_ref.shape[0]
    for j in range(nblk):
        cr = coef_re[:, j * S5_NBLK:(j + 1) * S5_NBLK]
        ci = coef_im[:, j * S5_NBLK:(j + 1) * S5_NBLK]
        br = br_ref[j]
        bi = bi_ref[j]
        bbr_ref[j] = (cr * br - ci * bi).astype(BF16)
        bbi_ref[j] = (cr * bi + ci * br).astype(BF16)
    gn = lr.shape[-1]
    row = lax.broadcasted_iota(jnp.int32, (SUBLANES, gn), 0)
    ab_ref[0] = jnp.broadcast_to(a_re, (SUBLANES, gn))
    ab_ref[1] = jnp.broadcast_to(a_im, (SUBLANES, gn))
    pr, pi = a_re, a_im
    for _ in range(S5_SEG.bit_length() - 1):
        pr, pi = pr * pr - pi * pi, 2.0 * pr * pi
    aseg_ref[0] = jnp.broadcast_to(pr, (SUBLANES, gn))
    aseg_ref[1] = jnp.broadcast_to(pi, (SUBLANES, gn))
    for s in range(3):
        k = 1 << s
        dbl_ref[s, 0] = jnp.where(row >= k, pr, 0.0)
        dbl_ref[s, 1] = jnp.where(row >= k, pi, 0.0)
        pr, pi = pr * pr - pi * pi, 2.0 * pr * pi


def _s5_prep(lam_re, lam_im, log_dt, b_re, b_im):
    g, n = lam_re.shape
    gn = g * n
    p = b_re.shape[-1]
    gpb = S5_KBLK // p
    nblk = g // gpb
    lr = lam_re.reshape(1, gn)
    li = lam_im.reshape(1, gn)
    ldt = jnp.repeat(log_dt, n).reshape(1, gn)
    eye = jnp.eye(gpb, dtype=F32)

    def blockdiag(b):
        bt = jnp.transpose(b, (0, 2, 1)).reshape(nblk, gpb, p, n)
        return jnp.einsum('jgqn,gh->jgqhn', bt, eye).reshape(nblk, gpb * p, gpb * n)

    out_shape = (
        jax.ShapeDtypeStruct((nblk, S5_KBLK, S5_NBLK), BF16),
        jax.ShapeDtypeStruct((nblk, S5_KBLK, S5_NBLK), BF16),
        jax.ShapeDtypeStruct((2, SUBLANES, gn), F32),
        jax.ShapeDtypeStruct((2, SUBLANES, gn), F32),
        jax.ShapeDtypeStruct((3, 2, SUBLANES, gn), F32),
    )
    return pl.pallas_call(_s5_prep_kernel, out_shape=out_shape, name="s5_prep")(
        lr, li, ldt, blockdiag(b_re), blockdiag(b_im))


def _hg_kernel(layer, x_ref, g_ref, w_ref, lbl_ref, ng_ref, o_ref, st_ref):
    @pl.when(pl.program_id(1) == 0)
    def _():
        st_ref[...] = jnp.zeros_like(st_ref)

    rows = x_ref.shape[0]
    nch = rows // HG_CHUNK
    width = o_ref.shape[-1]
    heads = width // HG_HEAD_DIM

    h = _rmsnorm(x_ref[...], g_ref[...]).astype(BF16)
    proj = _dot(h, w_ref[...])
    q_raw = proj[:, 0:width]
    f_raw = proj[:, width:2 * width]
    v = proj[:, 2 * width:3 * width]
    og = proj[:, 3 * width:4 * width]

    lb = jnp.sum(jax.nn.softmax(lbl_ref[...], axis=0)[0:layer + 1], axis=0, keepdims=True)
    f = lb + (1.0 - lb) * jax.nn.sigmoid(f_raw)
    k = 1.0 - f
    q = jax.nn.silu(q_raw)
    logf = jnp.log(f)

    r_i = lax.broadcasted_iota(jnp.int32, (HG_CHUNK, HG_CHUNK), 0)
    c_i = lax.broadcasted_iota(jnp.int32, (HG_CHUNK, HG_CHUNK), 1)
    causal = r_i >= c_i
    tri = jnp.broadcast_to(jnp.where(causal, 1.0, 0.0).astype(BF16), (nch, HG_CHUNK, HG_CHUNK))
    hi = logf.astype(BF16)
    r1 = logf - hi.astype(F32)
    mid = r1.astype(BF16)
    lo = (r1 - mid.astype(F32)).astype(BF16)
    b = None
    for part in (hi, mid, lo):
        t = jnp.einsum('cts,csn->ctn', tri, part.reshape(nch, HG_CHUNK, width),
                       preferred_element_type=F32)
        b = t if b is None else b + t

    b_mid = b[:, HG_CHUNK // 2:HG_CHUNK // 2 + 1, :]
    b_last = b[:, HG_CHUNK - 1:HG_CHUNK, :]
    q3 = q.reshape(nch, HG_CHUNK, width)
    k3 = k.reshape(nch, HG_CHUNK, width)
    v3 = v.reshape(nch, HG_CHUNK, width).astype(BF16)
    qe = (q3 * jnp.exp(b - b_mid)).astype(BF16)
    ke = (k3 * jnp.exp(b_mid - b)).astype(BF16)
    kl = (k3 * jnp.exp(b_last - b)).astype(BF16)
    qb = (q3 * jnp.exp(b)).astype(BF16)
    dec = jnp.exp(b_last)

    ng = ng_ref[...]
    for hd in range(heads):
        sl = slice(hd * HG_HEAD_DIM, (hd + 1) * HG_HEAD_DIM)
        s = jnp.einsum('cqd,ckd->cqk', qe[:, :, sl], ke[:, :, sl], preferred_element_type=F32)
        s = jnp.where(causal[None], s, 0.0)
        o_intra = jnp.einsum('cqk,cke->cqe', s.astype(BF16), v3[:, :, sl], preferred_element_type=F32)
        ut = jnp.einsum('cse,csd->ced', v3[:, :, sl], kl[:, :, sl], preferred_element_type=F32)
        st = st_ref[hd]
        outs = []
        for c in range(nch):
            o_inter = lax.dot_general(qb[c, :, sl], st.astype(BF16), (((1,), (1,)), ((), ())),
                                      preferred_element_type=F32)
            outs.append(o_intra[c] + o_inter)
            st = dec[c, :, sl] * st + ut[c]
        st_ref[hd] = st
        o = jnp.concatenate(outs, axis=0)
        o = o * lax.rsqrt(jnp.mean(o * o, axis=-1, keepdims=True) + EPS) * ng
        o_ref[:, sl] = (o * jax.nn.silu(og[:, sl])).astype(o_ref.dtype)


def _hg_mixer(layer, x, g_mix, w_hg, lb_logits, norm_g):
    bsz, seqlen, d = x.shape
    width = w_hg.shape[1] // 4
    heads = width // HG_HEAD_DIM
    ng = norm_g.reshape(1, HG_HEAD_DIM)
    grid = (bsz, seqlen // HG_TILE)
    return pl.pallas_call(
        functools.partial(_hg_kernel, layer),
        out_shape=jax.ShapeDtypeStruct((bsz, seqlen, width), BF16),
        grid=grid,
        in_specs=[
            pl.BlockSpec((None, HG_TILE, d), lambda b, t: (b, t, 0)),
            _const_spec((1, d)),
            _const_spec(w_hg.shape),
            _const_spec(lb_logits.shape),
            _const_spec((1, HG_HEAD_DIM)),
        ],
        out_specs=pl.BlockSpec((None, HG_TILE, width), lambda b, t: (b, t, 0)),
        scratch_shapes=[pltpu.VMEM((heads, HG_HEAD_DIM, HG_HEAD_DIM), F32)],
        compiler_params=pltpu.CompilerParams(
            dimension_semantics=("arbitrary", "arbitrary"), vmem_limit_bytes=VMEM_LIMIT_BYTES),
        name="hg_mixer",
    )(x, g_mix.reshape(1, d), w_hg, lb_logits, ng)


def _s5_kernel(x_ref, g_ref, wu_ref, bbr_ref, bbi_ref, ab_ref, aseg_ref, dbl_ref, cre_ref, cim_ref,
               dskip_ref, gw_ref, gb_ref, o_ref, us_ref, up_ref, xr_ref, xi_ref, car_ref, os_ref):
    @pl.when(pl.program_id(1) == 0)
    def _():
        car_ref[...] = jnp.zeros_like(car_ref)

    rows = x_ref.shape[0]
    width = o_ref.shape[-1]
    gn = xr_ref.shape[-1]
    nblk = bbr_ref.shape[0]
    ncol = gn // LANES
    nslab = width // LANES
    seg = rows // SUBLANES

    h = _rmsnorm(x_ref[...], g_ref[...]).astype(BF16)
    u = _dot(h, wu_ref[...])

    for j in range(nslab):
        us_ref[j] = u[:, j * LANES:(j + 1) * LANES]
    for m in range(seg):
        for j in range(nslab):
            up_ref[m * SUBLANES:(m + 1) * SUBLANES, j * LANES:(j + 1) * LANES] = (
                us_ref[j, pl.ds(m, SUBLANES, stride=seg), :])
    u = up_ref[...]
    ub = u.astype(BF16)
    for j in range(nblk):
        uj = ub[:, j * S5_KBLK:(j + 1) * S5_KBLK]
        xr_ref[:, j * S5_NBLK:(j + 1) * S5_NBLK] = _dot(uj, bbr_ref[j])
        xi_ref[:, j * S5_NBLK:(j + 1) * S5_NBLK] = _dot(uj, bbi_ref[j])

    def scan_step(m, carry, store):
        r0 = pl.multiple_of(m * SUBLANES, SUBLANES)
        new = []
        for c in range(ncol):
            cs = slice(c * LANES, (c + 1) * LANES)
            pr, pi = carry[2 * c], carry[2 * c + 1]
            ar, ai = ab_ref[0, :, cs], ab_ref[1, :, cs]
            re = ar * pr - ai * pi + xr_ref[pl.ds(r0, SUBLANES), cs]
            im = ar * pi + ai * pr + xi_ref[pl.ds(r0, SUBLANES), cs]
            if store:
                xr_ref[pl.ds(r0, SUBLANES), cs] = re
                xi_ref[pl.ds(r0, SUBLANES), cs] = im
            new += [re, im]
        return tuple(new)

    zero = jnp.zeros((SUBLANES, LANES), F32)
    ends = lax.fori_loop(0, seg, functools.partial(scan_step, store=False), (zero,) * (2 * ncol))

    row = lax.broadcasted_iota(jnp.int32, (SUBLANES, LANES), 0)
    starts = []
    for c in range(ncol):
        cs = slice(c * LANES, (c + 1) * LANES)
        er, ei = ends[2 * c], ends[2 * c + 1]
        sr = jnp.where(row == 0, pltpu.roll(car_ref[2 * c], 1, 0), pltpu.roll(er, 1, 0))
        si = jnp.where(row == 0, pltpu.roll(car_ref[2 * c + 1], 1, 0), pltpu.roll(ei, 1, 0))
        for s in range(3):
            ar, ai = dbl_ref[s, 0, :, cs], dbl_ref[s, 1, :, cs]
            qr = pltpu.roll(sr, 1 << s, 0)
            qi = pltpu.roll(si, 1 << s, 0)
            sr, si = sr + ar * qr - ai * qi, si + ar * qi + ai * qr
        gr, gi = aseg_ref[0, :, cs], aseg_ref[1, :, cs]
        car_ref[2 * c] = gr * sr - gi * si + er
        car_ref[2 * c + 1] = gr * si + gi * sr + ei
        starts += [sr, si]

    lax.fori_loop(0, seg, functools.partial(scan_step, store=True), tuple(starts))

    nout = cre_ref.shape[0]
    kout = cre_ref.shape[1]
    ys = []
    for m in range(nout):
        ks = slice(m * kout, (m + 1) * kout)
        ys.append(_dot(xr_ref[:, ks].astype(BF16), cre_ref[m])
                  - _dot(xi_ref[:, ks].astype(BF16), cim_ref[m]))
    y = jnp.concatenate(ys, axis=-1) + dskip_ref[...] * u
    gl = jax.nn.gelu(y)
    out = gl * jax.nn.sigmoid(_dot(gl.astype(BF16), gw_ref[...]) + gb_ref[...])
    for m in range(seg):
        for j in range(nslab):
            os_ref[j, pl.ds(m, SUBLANES, stride=seg), :] = (
                out[m * SUBLANES:(m + 1) * SUBLANES, j * LANES:(j + 1) * LANES])
    for j in range(nslab):
        o_ref[:, j * LANES:(j + 1) * LANES] = os_ref[j].astype(o_ref.dtype)


def _s5_mixer(x, g_mix, w_u, bbr, bbi, ab, aseg, dbl, c_re, c_im, d_skip, glu_w, glu_b):
    bsz, seqlen, d = x.shape
    width = w_u.shape[1]
    gn = ab.shape[-1]
    g, p, n = c_re.shape
    ncol = gn // LANES
    nslab = width // LANES
    gpo = 2 * LANES // p
    nout = g // gpo
    eye = jnp.eye(gpo, dtype=F32)

    def blockdiag(c):
        ct = jnp.transpose(c, (0, 2, 1)).reshape(nout, gpo, n, p)
        return jnp.einsum('mgnp,gh->mgnhp', ct, eye).reshape(nout, gpo * n, gpo * p).astype(BF16)

    grid = (bsz, seqlen // S5_TILE)
    return pl.pallas_call(
        _s5_kernel,
        out_shape=jax.ShapeDtypeStruct((bsz, seqlen, width), BF16),
        grid=grid,
        in_specs=[
            pl.BlockSpec((None, S5_TILE, d), lambda b, t: (b, t, 0)),
            _const_spec((1, d)),
            _const_spec(w_u.shape),
            _const_spec(bbr.shape),
            _const_spec(bbi.shape),
            _const_spec(ab.shape),
            _const_spec(aseg.shape),
            _const_spec(dbl.shape),
            _const_spec((nout, gpo * n, gpo * p)),
            _const_spec((nout, gpo * n, gpo * p)),
            _const_spec((1, width)),
            _const_spec(glu_w.shape),
            _const_spec((1, width)),
        ],
        out_specs=pl.BlockSpec((None, S5_TILE, width), lambda b, t: (b, t, 0)),
        scratch_shapes=[
            pltpu.VMEM((nslab, S5_TILE, LANES), F32),
            pltpu.VMEM((S5_TILE, width), F32),
            pltpu.VMEM((S5_TILE, gn), F32),
            pltpu.VMEM((S5_TILE, gn), F32),
            pltpu.VMEM((2 * ncol, SUBLANES, LANES), F32),
            pltpu.VMEM((nslab, S5_TILE, LANES), F32),
        ],
        compiler_params=pltpu.CompilerParams(
            dimension_semantics=("arbitrary", "arbitrary"), vmem_limit_bytes=VMEM_LIMIT_BYTES),
        name="s5_mixer",
    )(x, g_mix.reshape(1, d), w_u, bbr, bbi, ab, aseg, dbl, blockdiag(c_re), blockdiag(c_im),
      d_skip.reshape(1, width), glu_w, glu_b.reshape(1, width))


def _merge_kernel(x_ref, g_ref, wg_ref, ohg_ref, os5_ref, whg_ref, ws5_ref, wout_ref, o_ref):
    d = x_ref.shape[-1]
    x = x_ref[...]
    h = _rmsnorm(x, g_ref[...]).astype(BF16)
    gates = _dot(h, wg_ref[...])
    y_hg = _dot(ohg_ref[...], whg_ref[...])
    y_s5 = _dot(os5_ref[...], ws5_ref[...])
    merged = jax.nn.sigmoid(gates[:, :d]) * y_hg + jax.nn.sigmoid(gates[:, d:]) * y_s5
    o_ref[...] = x + _dot(merged.astype(BF16), wout_ref[...])


def _merge(x2, g_mix, w_gates, o_hg, o_s5, w_bhg, w_bs5, w_out):
    t, d = x2.shape
    width = o_hg.shape[-1]
    return pl.pallas_call(
        _merge_kernel,
        out_shape=jax.ShapeDtypeStruct((t, d), F32),
        grid=(t // MIX_TILE,),
        in_specs=[
            pl.BlockSpec((MIX_TILE, d), lambda i: (i, 0)),
            _const_spec((1, d)),
            _const_spec(w_gates.shape),
            pl.BlockSpec((MIX_TILE, width), lambda i: (i, 0)),
            pl.BlockSpec((MIX_TILE, width), lambda i: (i, 0)),
            _const_spec(w_bhg.shape),
            _const_spec(w_bs5.shape),
            _const_spec(w_out.shape),
        ],
        out_specs=pl.BlockSpec((MIX_TILE, d), lambda i: (i, 0)),
        compiler_params=pltpu.CompilerParams(
            dimension_semantics=("arbitrary",), vmem_limit_bytes=VMEM_LIMIT_BYTES),
        name="merge",
    )(x2, g_mix.reshape(1, d), w_gates, o_hg, o_s5, w_bhg, w_bs5, w_out)


def _ffn_kernel(x_ref, p_ref, gf_ref, wup_ref, cw_ref, cb_ref, wdn_ref, gp_ref, wpg_ref, wpp_ref,
                gl_ref, o_ref, a_ref):
    rows = x_ref.shape[0]
    dff = wdn_ref.shape[0]

    @pl.when(pl.program_id(1) == 0)
    def _():
        a_ref[0:CONV_HALO, :] = jnp.zeros((CONV_HALO, a_ref.shape[-1]), F32)

    x = x_ref[...]
    h = _rmsnorm(x, gf_ref[...]).astype(BF16)
    a_ref[CONV_HALO:CONV_HALO + rows, :] = _dot(h, wup_ref[...])
    c = cb_ref[...]
    for k in range(CONV_WIDTH):
        off = CONV_HALO - (CONV_WIDTH - 1) + k
        c = c + cw_ref[k:k + 1, :] * a_ref[off:off + rows, :]
    a_ref[0:CONV_HALO, :] = a_ref[rows:rows + CONV_HALO, :]
    act = (jax.nn.gelu(c[:, :dff]) * c[:, dff:]).astype(BF16)
    x = x + _dot(act, wdn_ref[...])
    hp = _rmsnorm(x, gp_ref[...]).astype(BF16)
    ple_gate = jax.nn.sigmoid(_dot(hp, wpg_ref[...]))
    x = x + ple_gate * _dot(p_ref[...].astype(BF16), wpp_ref[...])
    o_ref[...] = _rmsnorm(x, gl_ref[...])


def _ffn(x1, p, g_ffn, w_up, conv_w, conv_b, w_down, g_ple, w_pg, w_pp, g_final):
    bsz, seqlen, d = x1.shape
    pdim = p.shape[-1]
    dff2 = w_up.shape[1]
    row_spec = pl.BlockSpec((None, FFN_TILE, d), lambda b, t: (b, t, 0))
    return pl.pallas_call(
        _ffn_kernel,
        out_shape=jax.ShapeDtypeStruct((bsz, seqlen, d), F32),
        grid=(bsz, seqlen // FFN_TILE),
        in_specs=[
            row_spec,
            pl.BlockSpec((None, FFN_TILE, pdim), lambda b, t: (b, t, 0)),
            _const_spec((1, d)),
            _const_spec(w_up.shape),
            _const_spec(conv_w.shape),
            _const_spec((1, dff2)),
            _const_spec(w_down.shape),
            _const_spec((1, d)),
            _const_spec(w_pg.shape),
            _const_spec(w_pp.shape),
            _const_spec((1, d)),
        ],
        out_specs=row_spec,
        scratch_shapes=[pltpu.VMEM((CONV_HALO + FFN_TILE, dff2), F32)],
        compiler_params=pltpu.CompilerParams(
            dimension_semantics=("arbitrary", "arbitrary"), vmem_limit_bytes=VMEM_LIMIT_BYTES),
        name="ffn",
    )(x1, p, g_ffn.reshape(1, d), w_up, conv_w, conv_b.reshape(1, dff2), w_down,
      g_ple.reshape(1, d), w_pg, w_pp, g_final.reshape(1, d))


def kernel(x, p, norm_mix_g, w_in, hg_lb_logits, hg_norm_g, s5_lambda_re, s5_lambda_im, s5_log_dt, s5_b_re, s5_b_im, s5_c_re, s5_c_im, s5_d, s5_glu_w, s5_glu_b, w_branch_hg, w_branch_s5, w_out, norm_ffn_g, w_up, conv_w, conv_b, w_down, norm_ple_g, w_ple_gate, w_ple_proj, norm_final_g):
    depth = w_in.shape[0]
    bsz, seqlen, d = x.shape
    hgw = w_branch_hg.shape[1]
    s5w = w_branch_s5.shape[1]
    for i in range(depth):
        w_in_b = w_in[i].astype(BF16)
        w_hg = w_in_b[:, :4 * hgw]
        w_u = w_in_b[:, 4 * hgw:4 * hgw + s5w]
        w_gates = w_in_b[:, 4 * hgw + s5w:]
        bbr, bbi, ab, aseg, dbl = _s5_prep(s5_lambda_re[i], s5_lambda_im[i], s5_log_dt[i],
                                           s5_b_re[i], s5_b_im[i])
        o_hg = _hg_mixer(i, x, norm_mix_g[i], w_hg, hg_lb_logits, hg_norm_g[i])
        o_s5 = _s5_mixer(x, norm_mix_g[i], w_u, bbr, bbi, ab, aseg, dbl, s5_c_re[i], s5_c_im[i],
                         s5_d[i], s5_glu_w[i].astype(BF16), s5_glu_b[i])
        x1 = _merge(x.reshape(bsz * seqlen, d), norm_mix_g[i], w_gates,
                    o_hg.reshape(bsz * seqlen, hgw), o_s5.reshape(bsz * seqlen, s5w),
                    w_branch_hg[i].astype(BF16), w_branch_s5[i].astype(BF16), w_out[i].astype(BF16))
        assert i == depth - 1, "the final norm is fused into the FFN call, so only one layer is supported"
        x = _ffn(x1.reshape(bsz, seqlen, d), p[i], norm_ffn_g[i], w_up[i].astype(BF16), conv_w[i],
                 conv_b[i], w_down[i].astype(BF16), norm_ple_g[i], w_ple_gate[i].astype(BF16),
                 w_ple_proj[i].astype(BF16), norm_final_g)
    return x
```

```python
import functools

import jax
import jax.numpy as jnp
from jax import lax
from jax.experimental import pallas as pl
from jax.experimental.pallas import tpu as pltpu

F32 = jnp.float32
BF16 = jnp.bfloat16

EPS = 1e-6
HG_CHUNK = 64
HG_HEAD_DIM = 128
S5_GROUP = 16
S5_STATE = 64
CONV_WIDTH = 3

SUBLANES = 8
LANES = 128
VMEM_LIMIT_BYTES = 56 * 1024 * 1024

HG_TILE = 512
S5_TILE = 512
S5_SEG = S5_TILE // SUBLANES
assert S5_SEG & (S5_SEG - 1) == 0, "Abar^S5_SEG is built by repeated squaring"
MIX_TILE = 512
FFN_TILE = 256
S5_KBLK = 128
S5_NBLK = S5_KBLK // S5_GROUP * S5_STATE
CONV_HALO = SUBLANES


def _rmsnorm(x, g):
    return x * lax.rsqrt(jnp.mean(x * x, axis=-1, keepdims=True) + EPS) * g


def _dot(a, b):
    return jnp.dot(a, b, preferred_element_type=F32)


def _const_spec(shape):
    nd = len(shape)
    return pl.BlockSpec(shape, lambda *_: (0,) * nd)


def _s5_prep_kernel(lr_ref, li_ref, ldt_ref, br_ref, bi_ref,
                    bbr_ref, bbi_ref, ab_ref, aseg_ref, dbl_ref):
    lr = lr_ref[...]
    li = li_ref[...]
    dt = jnp.exp(ldt_ref[...])
    mag = jnp.exp(lr * dt)
    a_re = mag * jnp.cos(li * dt)
    a_im = mag * jnp.sin(li * dt)
    den = lr * lr + li * li
    coef_re = ((a_re - 1.0) * lr + a_im * li) / den
    coef_im = (a_im * lr - (a_re - 1.0) * li) / den
    nblk = br_ref.shape[0]
    for j in range(nblk):
        cr = coef_re[:, j * S5_NBLK:(j + 1) * S5_NBLK]
        ci = coef_im[:, j * S5_NBLK:(j + 1) * S5_NBLK]
        br = br_ref[j]
        bi = bi_ref[j]
        bbr_ref[j] = (cr * br - ci * bi).astype(BF16)
        bbi_ref[j] = (cr * bi + ci * br).astype(BF16)
    gn = lr.shape[-1]
    row = lax.broadcasted_iota(jnp.int32, (SUBLANES, gn), 0)
    ab_ref[0] = jnp.broadcast_to(a_re, (SUBLANES, gn))
    ab_ref[1] = jnp.broadcast_to(a_im, (SUBLANES, gn))
    pr, pi = a_re, a_im
    for _ in range(S5_SEG.bit_length() - 1):
        pr, pi = pr * pr - pi * pi, 2.0 * pr * pi
    aseg_ref[0] = jnp.broadcast_to(pr, (SUBLANES, gn))
    aseg_ref[1] = jnp.broadcast_to(pi, (SUBLANES, gn))
    for s in range(3):
        k = 1 << s
        dbl_ref[s, 0] = jnp.where(row >= k, pr, 0.0)
        dbl_ref[s, 1] = jnp.where(row >= k, pi, 0.0)
        pr, pi = pr * pr - pi * pi, 2.0 * pr * pi


def _s5_prep(lam_re, lam_im, log_dt, b_re, b_im):
    g, n = lam_re.shape
    gn = g * n
    p = b_re.shape[-1]
    gpb = S5_KBLK // p
    nblk = g // gpb
    lr = lam_re.reshape(1, gn)
    li = lam_im.reshape(1, gn)
    ldt = jnp.repeat(log_dt, n).reshape(1, gn)
    eye = jnp.eye(gpb, dtype=F32)

    def blockdiag(b):
        bt = jnp.transpose(b, (0, 2, 1)).reshape(nblk, gpb, p, n)
        return jnp.einsum('jgqn,gh->jgqhn', bt, eye).reshape(nblk, gpb * p, gpb * n)

    out_shape = (
        jax.ShapeDtypeStruct((nblk, S5_KBLK, S5_NBLK), BF16),
        jax.ShapeDtypeStruct((nblk, S5_KBLK, S5_NBLK), BF16),
        jax.ShapeDtypeStruct((2, SUBLANES, gn), F32),
        jax.ShapeDtypeStruct((2, SUBLANES, gn), F32),
        jax.ShapeDtypeStruct((3, 2, SUBLANES, gn), F32),
    )
    return pl.pallas_call(_s5_prep_kernel, out_shape=out_shape, name="s5_prep")(
        lr, li, ldt, blockdiag(b_re), blockdiag(b_im))


def _hg_kernel(layer, x_ref, g_ref, w_ref, lbl_ref, ng_ref, o_ref, st_ref):
    @pl.when(pl.program_id(1) == 0)
    def _():
        st_ref[...] = jnp.zeros_like(st_ref)

    rows = x_ref.shape[0]
    nch = rows // HG_CHUNK
    width = o_ref.shape[-1]
    heads = width // HG_HEAD_DIM

    h = _rmsnorm(x_ref[...], g_ref[...]).astype(BF16)
    proj = _dot(h, w_ref[...])
    q_raw = proj[:, 0:width]
    f_raw = proj[:, width:2 * width]
    v = proj[:, 2 * width:3 * width]
    og = proj[:, 3 * width:4 * width]

    lb = jnp.sum(jax.nn.softmax(lbl_ref[...], axis=0)[0:layer + 1], axis=0, keepdims=True)
    f = lb + (1.0 - lb) * jax.nn.sigmoid(f_raw)
    k = 1.0 - f
    q = jax.nn.silu(q_raw)
    logf = jnp.log(f)

    r_i = lax.broadcasted_iota(jnp.int32, (HG_CHUNK, HG_CHUNK), 0)
    c_i = lax.broadcasted_iota(jnp.int32, (HG_CHUNK, HG_CHUNK), 1)
    causal = r_i >= c_i
    tri = jnp.broadcast_to(jnp.where(causal, 1.0, 0.0).astype(BF16), (nch, HG_CHUNK, HG_CHUNK))
    hi = logf.astype(BF16)
    r1 = logf - hi.astype(F32)
    mid = r1.astype(BF16)
    lo = (r1 - mid.astype(F32)).astype(BF16)
    b = None
    for part in (hi, mid, lo):
        t = jnp.einsum('cts,csn->ctn', tri, part.reshape(nch, HG_CHUNK, width),
                       preferred_element_type=F32)
        b = t if b is None else b + t

    b_mid = b[:, HG_CHUNK // 2:HG_CHUNK // 2 + 1, :]
    b_last = b[:, HG_CHUNK - 1:HG_CHUNK, :]
    q3 = q.reshape(nch, HG_CHUNK, width)
    k3 = k.reshape(nch, HG_CHUNK, width)
    v3 = v.reshape(nch, HG_CHUNK, width).astype(BF16)
    qe = (q3 * jnp.exp(b - b_mid)).astype(BF16)
    ke = (k3 * jnp.exp(b_mid - b)).astype(BF16)
    kl = (k3 * jnp.exp(b_last - b)).astype(BF16)
    qb = (q3 * jnp.exp(b)).astype(BF16)
    dec = jnp.exp(b_last)

    ng = ng_ref[...]
    for hd in range(heads):
        sl = slice(hd * HG_HEAD_DIM, (hd + 1) * HG_HEAD_DIM)
        s = jnp.einsum('cqd,ckd->cqk', qe[:, :, sl], ke[:, :, sl], preferred_element_type=F32)
        s = jnp.where(causal[None], s, 0.0)
        o_intra = jnp.einsum('cqk,cke->cqe', s.astype(BF16), v3[:, :, sl], preferred_element_type=F32)
        ut = jnp.einsum('cse,csd->ced', v3[:, :, sl], kl[:, :, sl], preferred_element_type=F32)
        st = st_ref[hd]
        outs = []
        for c in range(nch):
            o_inter = lax.dot_general(qb[c, :, sl], st.astype(BF16), (((1,), (1,)), ((), ())),
                                      preferred_element_type=F32)
            outs.append(o_intra[c] + o_inter)
            st = dec[c, :, sl] * st + ut[c]
        st_ref[hd] = st
        o = jnp.concatenate(outs, axis=0)
        o = o * lax.rsqrt(jnp.mean(o * o, axis=-1, keepdims=True) + EPS) * ng
        o_ref[:, sl] = (o * jax.nn.silu(og[:, sl])).astype(o_ref.dtype)


def _hg_mixer(layer, x, g_mix, w_hg, lb_logits, norm_g):
    bsz, seqlen, d = x.shape
    width = w_hg.shape[1] // 4
    heads = width // HG_HEAD_DIM
    ng = norm_g.reshape(1, HG_HEAD_DIM)
    grid = (bsz, seqlen // HG_TILE)
    return pl.pallas_call(
        functools.partial(_hg_kernel, layer),
        out_shape=jax.ShapeDtypeStruct((bsz, seqlen, width), BF16),
        grid=grid,
        in_specs=[
            pl.BlockSpec((None, HG_TILE, d), lambda b, t: (b, t, 0)),
            _const_spec((1, d)),
            _const_spec(w_hg.shape),
            _const_spec(lb_logits.shape),
            _const_spec((1, HG_HEAD_DIM)),
        ],
        out_specs=pl.BlockSpec((None, HG_TILE, width), lambda b, t: (b, t, 0)),
        scratch_shapes=[pltpu.VMEM((heads, HG_HEAD_DIM, HG_HEAD_DIM), F32)],
        compiler_params=pltpu.CompilerParams(
            dimension_semantics=("arbitrary", "arbitrary"), vmem_limit_bytes=VMEM_LIMIT_BYTES),
        name="hg_mixer",
    )(x, g_mix.reshape(1, d), w_hg, lb_logits, ng)


def _s5_kernel(x_ref, g_ref, wu_ref, bbr_ref, bbi_ref, ab_ref, aseg_ref, dbl_ref, cre_ref, cim_ref,
               dskip_ref, gw_ref, gb_ref, o_ref, us_ref, up_ref, xr_ref, xi_ref, car_ref, os_ref):
    @pl.when(pl.program_id(1) == 0)
    def _():
        car_ref[...] = jnp.zeros_like(car_ref)

    rows = x_ref.shape[0]
    width = o_ref.shape[-1]
    gn = xr_ref.shape[-1]
    nblk = bbr_ref.shape[0]
    ncol = gn // LANES
    nslab = width // LANES
    seg = rows // SUBLANES

    h = _rmsnorm(x_ref[...], g_ref[...]).astype(BF16)
    u = _dot(h, wu_ref[...])

    for j in range(nslab):
        us_ref[j] = u[:, j * LANES:(j + 1) * LANES]
    for m in range(seg):
        for j in range(nslab):
            up_ref[m * SUBLANES:(m + 1) * SUBLANES, j * LANES:(j + 1) * LANES] = (
                us_ref[j, pl.ds(m, SUBLANES, stride=seg), :])
    u = up_ref[...]
    ub = u.astype(BF16)
    for j in range(nblk):
        uj = ub[:, j * S5_KBLK:(j + 1) * S5_KBLK]
        xr_ref[:, j * S5_NBLK:(j + 1) * S5_NBLK] = _dot(uj, bbr_ref[j])
        xi_ref[:, j * S5_NBLK:(j + 1) * S5_NBLK] = _dot(uj, bbi_ref[j])

    def scan_step(m, carry, store):
        r0 = m * SUBLANES
        new = []
        for c in range(ncol):
            cs = slice(c * LANES, (c + 1) * LANES)
            pr, pi = carry[2 * c], carry[2 * c + 1]
            ar, ai = ab_ref[0, :, cs], ab_ref[1, :, cs]
            re = ar * pr - ai * pi + xr_ref[pl.ds(r0, SUBLANES), cs]
            im = ar * pi + ai * pr + xi_ref[pl.ds(r0, SUBLANES), cs]
            if store:
                xr_ref[pl.ds(r0, SUBLANES), cs] = re
                xi_ref[pl.ds(r0, SUBLANES), cs] = im
            new += [re, im]
        return tuple(new)

    zero = jnp.zeros((SUBLANES, LANES), F32)
    ends = (zero,) * (2 * ncol)
    for m in range(seg):
        ends = scan_step(m, ends, store=False)

    row = lax.broadcasted_iota(jnp.int32, (SUBLANES, LANES), 0)
    starts = []
    for c in range(ncol):
        cs = slice(c * LANES, (c + 1) * LANES)
        er, ei = ends[2 * c], ends[2 * c + 1]
        sr = jnp.where(row == 0, pltpu.roll(car_ref[2 * c], 1, 0), pltpu.roll(er, 1, 0))
        si = jnp.where(row == 0, pltpu.roll(car_ref[2 * c + 1], 1, 0), pltpu.roll(ei, 1, 0))
        for s in range(3):
            ar, ai = dbl_ref[s, 0, :, cs], dbl_ref[s, 1, :, cs]
            qr = pltpu.roll(sr, 1 << s, 0)
            qi = pltpu.roll(si, 1 << s, 0)
            sr, si = sr + ar * qr - ai * qi, si + ar * qi + ai * qr
        gr, gi = aseg_ref[0, :, cs], aseg_ref[1, :, cs]
        car_ref[2 * c] = gr * sr - gi * si + er
        car_ref[2 * c + 1] = gr * si + gi * sr + ei
        starts += [sr, si]

    state = tuple(starts)
    for m in range(seg):
        state = scan_step(m, state, store=True)

    nout = cre_ref.shape[0]
    kout = cre_ref.shape[1]
    ys = []
    for m in range(nout):
        ks = slice(m * kout, (m + 1) * kout)
        ys.append(_dot(xr_ref[:, ks].astype(BF16), cre_ref[m])
                  - _dot(xi_ref[:, ks].astype(BF16), cim_ref[m]))
    y = jnp.concatenate(ys, axis=-1) + dskip_ref[...] * u
    gl = jax.nn.gelu(y)
    out = gl * jax.nn.sigmoid(_dot(gl.astype(BF16), gw_ref[...]) + gb_ref[...])
    for m in range(seg):
        for j in range(nslab):
            os_ref[j, pl.ds(m, SUBLANES, stride=seg), :] = (
                out[m * SUBLANES:(m + 1) * SUBLANES, j * LANES:(j + 1) * LANES])
    for j in range(nslab):
        o_ref[:, j * LANES:(j + 1) * LANES] = os_ref[j].astype(o_ref.dtype)


def _s5_mixer(x, g_mix, w_u, bbr, bbi, ab, aseg, dbl, c_re, c_im, d_skip, glu_w, glu_b):
    bsz, seqlen, d = x.shape
    width = w_u.shape[1]
    gn = ab.shape[-1]
    g, p, n = c_re.shape
    ncol = gn // LANES
    nslab = width // LANES
    gpo = 2 * LANES // p
    nout = g // gpo
    eye = jnp.eye(gpo, dtype=F32)

    def blockdiag(c):
        ct = jnp.transpose(c, (0, 2, 1)).reshape(nout, gpo, n, p)
        return jnp.einsum('mgnp,gh->mgnhp', ct, eye).reshape(nout, gpo * n, gpo * p).astype(BF16)

    grid = (bsz, seqlen // S5_TILE)
    return pl.pallas_call(
        _s5_kernel,
        out_shape=jax.ShapeDtypeStruct((bsz, seqlen, width), BF16),
        grid=grid,
        in_specs=[
            pl.BlockSpec((None, S5_TILE, d), lambda b, t: (b, t, 0)),
            _const_spec((1, d)),
            _const_spec(w_u.shape),
            _const_spec(bbr.shape),
            _const_spec(bbi.shape),
            _const_spec(ab.shape),
            _const_spec(aseg.shape),
            _const_spec(dbl.shape),
            _const_spec((nout, gpo * n, gpo * p)),
            _const_spec((nout, gpo * n, gpo * p)),
            _const_spec((1, width)),
            _const_spec(glu_w.shape),
            _const_spec((1, width)),
        ],
        out_specs=pl.BlockSpec((None, S5_TILE, width), lambda b, t: (b, t, 0)),
        scratch_shapes=[
            pltpu.VMEM((nslab, S5_TILE, LANES), F32),
            pltpu.VMEM((S5_TILE, width), F32),
            pltpu.VMEM((S5_TILE, gn), F32),
            pltpu.VMEM((S5_TILE, gn), F32),
            pltpu.VMEM((2 * ncol, SUBLANES, LANES), F32),
            pltpu.VMEM((nslab, S5_TILE, LANES), F32),
        ],
        compiler_params=pltpu.CompilerParams(
            dimension_semantics=("arbitrary", "arbitrary"), vmem_limit_bytes=VMEM_LIMIT_BYTES),
        name="s5_mixer",
    )(x, g_mix.reshape(1, d), w_u, bbr, bbi, ab, aseg, dbl, blockdiag(c_re), blockdiag(c_im),
      d_skip.reshape(1, width), glu_w, glu_b.reshape(1, width))


def _merge_kernel(x_ref, g_ref, wg_ref, ohg_ref, os5_ref, whg_ref, ws5_ref, wout_ref, o_ref):
    d = x_ref.shape[-1]
    x = x_ref[...]
    h = _rmsnorm(x, g_ref[...]).astype(BF16)
    gates = _dot(h, wg_ref[...])
    y_hg = _dot(ohg_ref[...], whg_ref[...])
    y_s5 = _dot(os5_ref[...], ws5_ref[...])
    merged = jax.nn.sigmoid(gates[:, :d]) * y_hg + jax.nn.sigmoid(gates[:, d:]) * y_s5
    o_ref[...] = x + _dot(merged.astype(BF16), wout_ref[...])


def _merge(x2, g_mix, w_gates, o_hg, o_s5, w_bhg, w_bs5, w_out):
    t, d = x2.shape
    width = o_hg.shape[-1]
    return pl.pallas_call(
        _merge_kernel,
        out_shape=jax.ShapeDtypeStruct((t, d), F32),
        grid=(t // MIX_TILE,),
        in_specs=[
            pl.BlockSpec((MIX_TILE, d), lambda i: (i, 0)),
            _const_spec((1, d)),
            _const_spec(w_gates.shape),
            pl.BlockSpec((MIX_TILE, width), lambda i: (i, 0)),
            pl.BlockSpec((MIX_TILE, width), lambda i: (i, 0)),
            _const_spec(w_bhg.shape),
            _const_spec(w_bs5.shape),
            _const_spec(w_out.shape),
        ],
        out_specs=pl.BlockSpec((MIX_TILE, d), lambda i: (i, 0)),
        compiler_params=pltpu.CompilerParams(
            dimension_semantics=("arbitrary",), vmem_limit_bytes=VMEM_LIMIT_BYTES),
        name="merge",
    )(x2, g_mix.reshape(1, d), w_gates, o_hg, o_s5, w_bhg, w_bs5, w_out)


def _ffn_kernel(x_ref, p_ref, gf_ref, wup_ref, cw_ref, cb_ref, wdn_ref, gp_ref, wpg_ref, wpp_ref,
                gl_ref, o_ref, a_ref):
    rows = x_ref.shape[0]
    dff = wdn_ref.shape[0]

    @pl.when(pl.program_id(1) == 0)
    def _():
        a_ref[0:CONV_HALO, :] = jnp.zeros((CONV_HALO, a_ref.shape[-1]), F32)

    x = x_ref[...]
    h = _rmsnorm(x, gf_ref[...]).astype(BF16)
    a_ref[CONV_HALO:CONV_HALO + rows, :] = _dot(h, wup_ref[...])
    c = cb_ref[...]
    for k in range(CONV_WIDTH):
        off = CONV_HALO - (CONV_WIDTH - 1) + k
        c = c + cw_ref[k:k + 1, :] * a_ref[off:off + rows, :]
    a_ref[0:CONV_HALO, :] = a_ref[rows:rows + CONV_HALO, :]
    act = (jax.nn.gelu(c[:, :dff]) * c[:, dff:]).astype(BF16)
    x = x + _dot(act, wdn_ref[...])
    hp = _rmsnorm(x, gp_ref[...]).astype(BF16)
    ple_gate = jax.nn.sigmoid(_dot(hp, wpg_ref[...]))
    x = x + ple_gate * _dot(p_ref[...].astype(BF16), wpp_ref[...])
    o_ref[...] = _rmsnorm(x, gl_ref[...])


def _ffn(x1, p, g_ffn, w_up, conv_w, conv_b, w_down, g_ple, w_pg, w_pp, g_final):
    bsz, seqlen, d = x1.shape
    pdim = p.shape[-1]
    dff2 = w_up.shape[1]
    row_spec = pl.BlockSpec((None, FFN_TILE, d), lambda b, t: (b, t, 0))
    return pl.pallas_call(
        _ffn_kernel,
        out_shape=jax.ShapeDtypeStruct((bsz, seqlen, d), F32),
        grid=(bsz, seqlen // FFN_TILE),
        in_specs=[
            row_spec,
            pl.BlockSpec((None, FFN_TILE, pdim), lambda b, t: (b, t, 0)),
            _const_spec((1, d)),
            _const_spec(w_up.shape),
            _const_spec(conv_w.shape),
            _const_spec((1, dff2)),
            _const_spec(w_down.shape),
            _const_spec((1, d)),
            _const_spec(w_pg.shape),
            _const_spec(w_pp.shape),
            _const_spec((1, d)),
        ],
        out_specs=row_spec,
        scratch_shapes=[pltpu.VMEM((CONV_HALO + FFN_TILE, dff2), F32)],
        compiler_params=pltpu.CompilerParams(
            dimension_semantics=("arbitrary", "arbitrary"), vmem_limit_bytes=VMEM_LIMIT_BYTES),
        name="ffn",
    )(x1, p, g_ffn.reshape(1, d), w_up, conv_w, conv_b.reshape(1, dff2), w_down,
      g_ple.reshape(1, d), w_pg, w_pp, g_final.reshape(1, d))


def kernel(x, p, norm_mix_g, w_in, hg_lb_logits, hg_norm_g, s5_lambda_re, s5_lambda_im, s5_log_dt, s5_b_re, s5_b_im, s5_c_re, s5_c_im, s5_d, s5_glu_w, s5_glu_b, w_branch_hg, w_branch_s5, w_out, norm_ffn_g, w_up, conv_w, conv_b, w_down, norm_ple_g, w_ple_gate, w_ple_proj, norm_final_g):
    depth = w_in.shape[0]
    bsz, seqlen, d = x.shape
    hgw = w_branch_hg.shape[1]
    s5w = w_branch_s5.shape[1]
    for i in range(depth):
        w_in_b = w_in[i].astype(BF16)
        w_hg = w_in_b[:, :4 * hgw]
        w_u = w_in_b[:, 4 * hgw:4 * hgw + s5w]
        w_gates = w_in_b[:, 4 * hgw + s5w:]
        bbr, bbi, ab, aseg, dbl = _s5_prep(s5_lambda_re[i], s5_lambda_im[i], s5_log_dt[i],
                                           s5_b_re[i], s5_b_im[i])
        o_hg = _hg_mixer(i, x, norm_mix_g[i], w_hg, hg_lb_logits, hg_norm_g[i])
        o_s5 = _s5_mixer(x, norm_mix_g[i], w_u, bbr, bbi, ab, aseg, dbl, s5_c_re[i], s5_c_im[i],
                         s5_d[i], s5_glu_w[i].astype(BF16), s5_glu_b[i])
        x1 = _merge(x.reshape(bsz * seqlen, d), norm_mix_g[i], w_gates,
                    o_hg.reshape(bsz * seqlen, hgw), o_s5.reshape(bsz * seqlen, s5w),
                    w_branch_hg[i].astype(BF16), w_branch_s5[i].astype(BF16), w_out[i].astype(BF16))
        assert i == depth - 1, "the final norm is fused into the FFN call, so only one layer is supported"
        x = _ffn(x1.reshape(bsz, seqlen, d), p[i], norm_ffn_g[i], w_up[i].astype(BF16), conv_w[i],
                 conv_b[i], w_down[i].astype(BF16), norm_ple_g[i], w_ple_gate[i].astype(BF16),
                 w_ple_proj[i].astype(BF16), norm_final_g)
    return x
```

```python
import functools

import jax
import jax.numpy as jnp
from jax import lax
from jax.experimental import pallas as pl
from jax.experimental.pallas import tpu as pltpu

F32 = jnp.float32
BF16 = jnp.bfloat16

EPS = 1e-6
HG_CHUNK = 64
HG_HEAD_DIM = 128
S5_GROUP = 16
S5_STATE = 64
CONV_WIDTH = 3

SUBLANES = 8
LANES = 128
VMEM_LIMIT_BYTES = 56 * 1024 * 1024

MIX_TILE = 512
FFN_TILE = 256
S5_SEG = MIX_TILE // SUBLANES
assert S5_SEG & (S5_SEG - 1) == 0, "Abar^S5_SEG is built by repeated squaring"
S5_KBLK = 128
S5_NBLK = S5_KBLK // S5_GROUP * S5_STATE
CONV_HALO = SUBLANES


def _rmsnorm(x, g):
    return x * lax.rsqrt(jnp.mean(x * x, axis=-1, keepdims=True) + EPS) * g


def _dot(a, b):
    return jnp.dot(a, b, preferred_element_type=F32)


def _const_spec(shape):
    nd = len(shape)
    return pl.BlockSpec(shape, lambda *_: (0,) * nd, pipeline_mode=pl.Buffered(1))


def _s5_prep_kernel(lr_ref, li_ref, ldt_ref, br_ref, bi_ref,
                    bbr_ref, bbi_ref, ab_ref, aseg_ref, dbl_ref):
    lr = lr_ref[...]
    li = li_ref[...]
    dt = jnp.exp(ldt_ref[...])
    mag = jnp.exp(lr * dt)
    a_re = mag * jnp.cos(li * dt)
    a_im = mag * jnp.sin(li * dt)
    den = lr * lr + li * li
    coef_re = ((a_re - 1.0) * lr + a_im * li) / den
    coef_im = (a_im * lr - (a_re - 1.0) * li) / den
    nblk = br_ref.shape[0]
    for j in range(nblk):
        cr = coef_re[:, j * S5_NBLK:(j + 1) * S5_NBLK]
        ci = coef_im[:, j * S5_NBLK:(j + 1) * S5_NBLK]
        br = br_ref[j]
        bi = bi_ref[j]
        bbr_ref[j] = (cr * br - ci * bi).astype(BF16)
        bbi_ref[j] = (cr * bi + ci * br).astype(BF16)
    gn = lr.shape[-1]
    row = lax.broadcasted_iota(jnp.int32, (SUBLANES, gn), 0)
    ab_ref[0] = jnp.broadcast_to(a_re, (SUBLANES, gn))
    ab_ref[1] = jnp.broadcast_to(a_im, (SUBLANES, gn))
    pr, pi = a_re, a_im
    for _ in range(S5_SEG.bit_length() - 1):
        pr, pi = pr * pr - pi * pi, 2.0 * pr * pi
    aseg_ref[0] = jnp.broadcast_to(pr, (SUBLANES, gn))
    aseg_ref[1] = jnp.broadcast_to(pi, (SUBLANES, gn))
    for s in range(3):
        k = 1 << s
        dbl_ref[s, 0] = jnp.where(row >= k, pr, 0.0)
        dbl_ref[s, 1] = jnp.where(row >= k, pi, 0.0)
        pr, pi = pr * pr - pi * pi, 2.0 * pr * pi


def _s5_prep(lam_re, lam_im, log_dt, b_re, b_im):
    g, n = lam_re.shape
    gn = g * n
    p = b_re.shape[-1]
    gpb = S5_KBLK // p
    nblk = g // gpb
    lr = lam_re.reshape(1, gn)
    li = lam_im.reshape(1, gn)
    ldt = jnp.repeat(log_dt, n).reshape(1, gn)
    eye = jnp.eye(gpb, dtype=F32)

    def blockdiag(b):
        bt = jnp.transpose(b, (0, 2, 1)).reshape(nblk, gpb, p, n)
        return jnp.einsum('jgqn,gh->jgqhn', bt, eye).reshape(nblk, gpb * p, gpb * n)

    out_shape = (
        jax.ShapeDtypeStruct((nblk, S5_KBLK, S5_NBLK), BF16),
        jax.ShapeDtypeStruct((nblk, S5_KBLK, S5_NBLK), BF16),
        jax.ShapeDtypeStruct((2, SUBLANES, gn), F32),
        jax.ShapeDtypeStruct((2, SUBLANES, gn), F32),
        jax.ShapeDtypeStruct((3, 2, SUBLANES, gn), F32),
    )
    return pl.pallas_call(_s5_prep_kernel, out_shape=out_shape, name="s5_prep")(
        lr, li, ldt, blockdiag(b_re), blockdiag(b_im))


def _hgrn2(layer, proj, lbl_ref, ng_ref, st_ref):
    rows = proj.shape[0]
    width = proj.shape[1] // 4
    nch = rows // HG_CHUNK
    heads = width // HG_HEAD_DIM
    q_raw = proj[:, 0:width]
    f_raw = proj[:, width:2 * width]
    v = proj[:, 2 * width:3 * width]
    og = proj[:, 3 * width:4 * width]

    lb = jnp.sum(jax.nn.softmax(lbl_ref[...], axis=0)[0:layer + 1], axis=0, keepdims=True)
    f = lb + (1.0 - lb) * jax.nn.sigmoid(f_raw)
    k = 1.0 - f
    q = jax.nn.silu(q_raw)
    logf = jnp.log(f)

    r_i = lax.broadcasted_iota(jnp.int32, (HG_CHUNK, HG_CHUNK), 0)
    c_i = lax.broadcasted_iota(jnp.int32, (HG_CHUNK, HG_CHUNK), 1)
    causal = r_i >= c_i
    tri = jnp.broadcast_to(jnp.where(causal, 1.0, 0.0).astype(BF16), (nch, HG_CHUNK, HG_CHUNK))
    hi = logf.astype(BF16)
    r1 = logf - hi.astype(F32)
    mid = r1.astype(BF16)
    lo = (r1 - mid.astype(F32)).astype(BF16)
    b = None
    for part in (hi, mid, lo):
        t = jnp.einsum('cts,csn->ctn', tri, part.reshape(nch, HG_CHUNK, width),
                       preferred_element_type=F32)
        b = t if b is None else b + t

    b_mid = b[:, HG_CHUNK // 2:HG_CHUNK // 2 + 1, :]
    b_last = b[:, HG_CHUNK - 1:HG_CHUNK, :]
    q3 = q.reshape(nch, HG_CHUNK, width)
    k3 = k.reshape(nch, HG_CHUNK, width)
    v3 = v.reshape(nch, HG_CHUNK, width).astype(BF16)
    qe = (q3 * jnp.exp(b - b_mid)).astype(BF16)
    ke = (k3 * jnp.exp(b_mid - b)).astype(BF16)
    kl = (k3 * jnp.exp(b_last - b)).astype(BF16)
    qb = (q3 * jnp.exp(b)).astype(BF16)
    dec = jnp.exp(b_last)

    ng = ng_ref[...]
    res = []
    for hd in range(heads):
        sl = slice(hd * HG_HEAD_DIM, (hd + 1) * HG_HEAD_DIM)
        s = jnp.einsum('cqd,ckd->cqk', qe[:, :, sl], ke[:, :, sl], preferred_element_type=F32)
        s = jnp.where(causal[None], s, 0.0)
        o_intra = jnp.einsum('cqk,cke->cqe', s.astype(BF16), v3[:, :, sl], preferred_element_type=F32)
        ut = jnp.einsum('cse,csd->ced', v3[:, :, sl], kl[:, :, sl], preferred_element_type=F32)
        st = st_ref[hd]
        outs = []
        for c in range(nch):
            o_inter = lax.dot_general(qb[c, :, sl], st.astype(BF16), (((1,), (1,)), ((), ())),
                                      preferred_element_type=F32)
            outs.append(o_intra[c] + o_inter)
            st = dec[c, :, sl] * st + ut[c]
        st_ref[hd] = st
        o = jnp.concatenate(outs, axis=0)
        o = o * lax.rsqrt(jnp.mean(o * o, axis=-1, keepdims=True) + EPS) * ng
        res.append(o * jax.nn.silu(og[:, sl]))
    return jnp.concatenate(res, axis=-1)


def _s5(u, bbr_ref, bbi_ref, ab_ref, aseg_ref, dbl_ref, cre_ref, cim_ref, dskip_ref, gw_ref, gb_ref,
        us_ref, up_ref, xr_ref, xi_ref, car_ref, os_ref):
    rows, width = u.shape
    gn = xr_ref.shape[-1]
    nblk = bbr_ref.shape[0]
    ncol = gn // LANES
    nslab = width // LANES
    seg = rows // SUBLANES

    for j in range(nslab):
        us_ref[j] = u[:, j * LANES:(j + 1) * LANES]
    for m in range(seg):
        for j in range(nslab):
            up_ref[m * SUBLANES:(m + 1) * SUBLANES, j * LANES:(j + 1) * LANES] = (
                us_ref[j, pl.ds(m, SUBLANES, stride=seg), :])
    u = up_ref[...]
    ub = u.astype(BF16)
    for j in range(nblk):
        uj = ub[:, j * S5_KBLK:(j + 1) * S5_KBLK]
        xr_ref[:, j * S5_NBLK:(j + 1) * S5_NBLK] = _dot(uj, bbr_ref[j])
        xi_ref[:, j * S5_NBLK:(j + 1) * S5_NBLK] = _dot(uj, bbi_ref[j])

    def scan_step(m, carry, store):
        r0 = m * SUBLANES
        new = []
        for c in range(ncol):
            cs = slice(c * LANES, (c + 1) * LANES)
            pr, pi = carry[2 * c], carry[2 * c + 1]
            ar, ai = ab_ref[0, :, cs], ab_ref[1, :, cs]
            re = ar * pr - ai * pi + xr_ref[pl.ds(r0, SUBLANES), cs]
            im = ar * pi + ai * pr + xi_ref[pl.ds(r0, SUBLANES), cs]
            if store:
                xr_ref[pl.ds(r0, SUBLANES), cs] = re
                xi_ref[pl.ds(r0, SUBLANES), cs] = im
            new += [re, im]
        return tuple(new)

    zero = jnp.zeros((SUBLANES, LANES), F32)
    ends = (zero,) * (2 * ncol)
    for m in range(seg):
        ends = scan_step(m, ends, store=False)

    row = lax.broadcasted_iota(jnp.int32, (SUBLANES, LANES), 0)
    starts = []
    for c in range(ncol):
        cs = slice(c * LANES, (c + 1) * LANES)
        er, ei = ends[2 * c], ends[2 * c + 1]
        sr = jnp.where(row == 0, pltpu.roll(car_ref[2 * c], 1, 0), pltpu.roll(er, 1, 0))
        si = jnp.where(row == 0, pltpu.roll(car_ref[2 * c + 1], 1, 0), pltpu.roll(ei, 1, 0))
        for s in range(3):
            ar, ai = dbl_ref[s, 0, :, cs], dbl_ref[s, 1, :, cs]
            qr = pltpu.roll(sr, 1 << s, 0)
            qi = pltpu.roll(si, 1 << s, 0)
            sr, si = sr + ar * qr - ai * qi, si + ar * qi + ai * qr
        gr, gi = aseg_ref[0, :, cs], aseg_ref[1, :, cs]
        car_ref[2 * c] = gr * sr - gi * si + er
        car_ref[2 * c + 1] = gr * si + gi * sr + ei
        starts += [sr, si]

    state = tuple(starts)
    for m in range(seg):
        state = scan_step(m, state, store=True)

    nout = cre_ref.shape[0]
    kout = cre_ref.shape[1]
    ys = []
    for m in range(nout):
        ks = slice(m * kout, (m + 1) * kout)
        ys.append(_dot(xr_ref[:, ks].astype(BF16), cre_ref[m])
                  - _dot(xi_ref[:, ks].astype(BF16), cim_ref[m]))
    y = jnp.concatenate(ys, axis=-1) + dskip_ref[...] * u
    gl = jax.nn.gelu(y)
    out = gl * jax.nn.sigmoid(_dot(gl.astype(BF16), gw_ref[...]) + gb_ref[...])
    for m in range(seg):
        for j in range(nslab):
            os_ref[j, pl.ds(m, SUBLANES, stride=seg), :] = (
                out[m * SUBLANES:(m + 1) * SUBLANES, j * LANES:(j + 1) * LANES])
    return jnp.concatenate([os_ref[j] for j in range(nslab)], axis=-1)


def _mixer_kernel(layer, x_ref, g_ref, win_ref, lbl_ref, ng_ref,
                  bbr_ref, bbi_ref, ab_ref, aseg_ref, dbl_ref, cre_ref, cim_ref, dskip_ref, gw_ref, gb_ref,
                  whg_ref, ws5_ref, wout_ref, o_ref,
                  st_ref, us_ref, up_ref, xr_ref, xi_ref, car_ref, os_ref):
    @pl.when(pl.program_id(1) == 0)
    def _():
        st_ref[...] = jnp.zeros_like(st_ref)
        car_ref[...] = jnp.zeros_like(car_ref)

    d = x_ref.shape[-1]
    hgw = whg_ref.shape[0]
    s5w = ws5_ref.shape[0]
    x = x_ref[...]
    h = _rmsnorm(x, g_ref[...]).astype(BF16)
    proj = _dot(h, win_ref[...])
    o_hg = _hgrn2(layer, proj[:, :4 * hgw], lbl_ref, ng_ref, st_ref)
    o_s5 = _s5(proj[:, 4 * hgw:4 * hgw + s5w], bbr_ref, bbi_ref, ab_ref, aseg_ref, dbl_ref,
               cre_ref, cim_ref, dskip_ref, gw_ref, gb_ref,
               us_ref, up_ref, xr_ref, xi_ref, car_ref, os_ref)
    gates = proj[:, 4 * hgw + s5w:]
    y_hg = _dot(o_hg.astype(BF16), whg_ref[...])
    y_s5 = _dot(o_s5.astype(BF16), ws5_ref[...])
    merged = jax.nn.sigmoid(gates[:, :d]) * y_hg + jax.nn.sigmoid(gates[:, d:]) * y_s5
    o_ref[...] = x + _dot(merged.astype(BF16), wout_ref[...])


def _mixer(layer, x, g_mix, w_in, lb_logits, norm_g, bbr, bbi, ab, aseg, dbl, c_re, c_im, d_skip,
           glu_w, glu_b, w_bhg, w_bs5, w_out):
    bsz, seqlen, d = x.shape
    hgw = w_bhg.shape[0]
    s5w = w_bs5.shape[0]
    heads = hgw // HG_HEAD_DIM
    gn = ab.shape[-1]
    g, p, n = c_re.shape
    ncol = gn // LANES
    nslab = s5w // LANES
    gpo = 2 * LANES // p
    nout = g // gpo
    eye = jnp.eye(gpo, dtype=F32)

    def blockdiag(c):
        ct = jnp.transpose(c, (0, 2, 1)).reshape(nout, gpo, n, p)
        return jnp.einsum('mgnp,gh->mgnhp', ct, eye).reshape(nout, gpo * n, gpo * p).astype(BF16)

    row_spec = pl.BlockSpec((None, MIX_TILE, d), lambda b, t: (b, t, 0))
    operands = (x, g_mix.reshape(1, d), w_in, lb_logits, norm_g.reshape(1, HG_HEAD_DIM),
                bbr, bbi, ab, aseg, dbl, blockdiag(c_re), blockdiag(c_im), d_skip.reshape(1, s5w),
                glu_w, glu_b.reshape(1, s5w), w_bhg, w_bs5, w_out)
    return pl.pallas_call(
        functools.partial(_mixer_kernel, layer),
        out_shape=jax.ShapeDtypeStruct((bsz, seqlen, d), F32),
        grid=(bsz, seqlen // MIX_TILE),
        in_specs=[row_spec] + [_const_spec(a.shape) for a in operands[1:]],
        out_specs=row_spec,
        scratch_shapes=[
            pltpu.VMEM((heads, HG_HEAD_DIM, HG_HEAD_DIM), F32),
            pltpu.VMEM((nslab, MIX_TILE, LANES), F32),
            pltpu.VMEM((MIX_TILE, s5w), F32),
            pltpu.VMEM((MIX_TILE, gn), F32),
            pltpu.VMEM((MIX_TILE, gn), F32),
            pltpu.VMEM((2 * ncol, SUBLANES, LANES), F32),
            pltpu.VMEM((nslab, MIX_TILE, LANES), F32),
        ],
        compiler_params=pltpu.CompilerParams(
            dimension_semantics=("arbitrary", "arbitrary"), vmem_limit_bytes=VMEM_LIMIT_BYTES),
        name="mixer",
    )(*operands)


def _ffn_kernel(x_ref, p_ref, gf_ref, wup_ref, cw_ref, cb_ref, wdn_ref, gp_ref, wpg_ref, wpp_ref,
                gl_ref, o_ref, a_ref):
    rows = x_ref.shape[0]
    dff = wdn_ref.shape[0]

    @pl.when(pl.program_id(1) == 0)
    def _():
        a_ref[0:CONV_HALO, :] = jnp.zeros((CONV_HALO, a_ref.shape[-1]), F32)

    x = x_ref[...]
    h = _rmsnorm(x, gf_ref[...]).astype(BF16)
    a_ref[CONV_HALO:CONV_HALO + rows, :] = _dot(h, wup_ref[...])
    c = cb_ref[...]
    for k in range(CONV_WIDTH):
        off = CONV_HALO - (CONV_WIDTH - 1) + k
        c = c + cw_ref[k:k + 1, :] * a_ref[off:off + rows, :]
    a_ref[0:CONV_HALO, :] = a_ref[rows:rows + CONV_HALO, :]
    act = (jax.nn.gelu(c[:, :dff]) * c[:, dff:]).astype(BF16)
    x = x + _dot(act, wdn_ref[...])
    hp = _rmsnorm(x, gp_ref[...]).astype(BF16)
    ple_gate = jax.nn.sigmoid(_dot(hp, wpg_ref[...]))
    x = x + ple_gate * _dot(p_ref[...].astype(BF16), wpp_ref[...])
    o_ref[...] = _rmsnorm(x, gl_ref[...])


def _ffn(x1, p, g_ffn, w_up, conv_w, conv_b, w_down, g_ple, w_pg, w_pp, g_final):
    bsz, seqlen, d = x1.shape
    pdim = p.shape[-1]
    dff2 = w_up.shape[1]
    row_spec = pl.BlockSpec((None, FFN_TILE, d), lambda b, t: (b, t, 0))
    return pl.pallas_call(
        _ffn_kernel,
        out_shape=jax.ShapeDtypeStruct((bsz, seqlen, d), F32),
        grid=(bsz, seqlen // FFN_TILE),
        in_specs=[
            row_spec,
            pl.BlockSpec((None, FFN_TILE, pdim), lambda b, t: (b, t, 0)),
            _const_spec((1, d)),
            _const_spec(w_up.shape),
            _const_spec(conv_w.shape),
            _const_spec((1, dff2)),
            _const_spec(w_down.shape),
            _const_spec((1, d)),
            _const_spec(w_pg.shape),
            _const_spec(w_pp.shape),
            _const_spec((1, d)),
        ],
        out_specs=row_spec,
        scratch_shapes=[pltpu.VMEM((CONV_HALO + FFN_TILE, dff2), F32)],
        compiler_params=pltpu.CompilerParams(
            dimension_semantics=("arbitrary", "arbitrary"), vmem_limit_bytes=VMEM_LIMIT_BYTES),
        name="ffn",
    )(x1, p, g_ffn.reshape(1, d), w_up, conv_w, conv_b.reshape(1, dff2), w_down,
      g_ple.reshape(1, d), w_pg, w_pp, g_final.reshape(1, d))


def kernel(x, p, norm_mix_g, w_in, hg_lb_logits, hg_norm_g, s5_lambda_re, s5_lambda_im, s5_log_dt, s5_b_re, s5_b_im, s5_c_re, s5_c_im, s5_d, s5_glu_w, s5_glu_b, w_branch_hg, w_branch_s5, w_out, norm_ffn_g, w_up, conv_w, conv_b, w_down, norm_ple_g, w_ple_gate, w_ple_proj, norm_final_g):
    depth = w_in.shape[0]
    for i in range(depth):
        bbr, bbi, ab, aseg, dbl = _s5_prep(s5_lambda_re[i], s5_lambda_im[i], s5_log_dt[i],
                                           s5_b_re[i], s5_b_im[i])
        x1 = _mixer(i, x, norm_mix_g[i], w_in[i].astype(BF16), hg_lb_logits, hg_norm_g[i],
                    bbr, bbi, ab, aseg, dbl, s5_c_re[i], s5_c_im[i], s5_d[i],
                    s5_glu_w[i].astype(BF16), s5_glu_b[i],
                    w_branch_hg[i].astype(BF16), w_branch_s5[i].astype(BF16), w_out[i].astype(BF16))
        assert i == depth - 1, "the final norm is fused into the FFN call, so only one layer is supported"
        x = _ffn(x1, p[i], norm_ffn_g[i], w_up[i].astype(BF16), conv_w[i],
                 conv_b[i], w_down[i].astype(BF16), norm_ple_g[i], w_ple_gate[i].astype(BF16),
                 w_ple_proj[i].astype(BF16), norm_final_g)
    return x
```
